```python
import jax, jax.numpy as jnp
from jax import lax
import numpy as np

D_MODEL = 1024
BATCH = 32
SEQ = 2048
DEPTH = 1
DEC_BATCH = 16
DEC_SEQ = 64
PAST_LEN = 4096

CHUNK = 64
Q_BLOCK = 128
D_MIX = D_MODEL
SB_WIDTH = D_MIX // 2
MLA_WIDTH = D_MIX - SB_WIDTH
SB_HEAD_DIM = 64
SB_HEADS = SB_WIDTH // SB_HEAD_DIM
MLA_V_DIM = 64
MLA_HEADS = MLA_WIDTH // MLA_V_DIM
MLA_NOPE_DIM = 64
MLA_ROPE_DIM = 32
Q_LORA_RANK = 3 * D_MODEL // 8
KV_LORA_RANK = D_MODEL // 4
ROPE_THETA = 10000.0
EPS = 1e-6
IN_SIZES = (SB_WIDTH, SB_WIDTH, SB_WIDTH, SB_WIDTH,
            Q_LORA_RANK, KV_LORA_RANK, MLA_ROPE_DIM, MLA_WIDTH)
IN_COLS = SB_WIDTH * 4 + Q_LORA_RANK + KV_LORA_RANK + MLA_ROPE_DIM + MLA_WIDTH

kernel_name = "sb_mla_parallel_stream_encoder_step"


def rms_norm(x, g):
    xf = x.astype(jnp.float32)
    y = xf * lax.rsqrt(jnp.mean(xf * xf, axis=-1, keepdims=True) + EPS)
    return (y * g.astype(jnp.float32)).astype(x.dtype)


def rope(x, pos):
    r = x.shape[-1]
    inv_freq = ROPE_THETA ** (-jnp.arange(0, r, 2, dtype=jnp.float32) / r)
    ang = pos.astype(jnp.float32)[:, None] * inv_freq[None, :]
    cos = jnp.cos(ang)[:, None, :]
    sin = jnp.sin(ang)[:, None, :]
    xf = x.astype(jnp.float32)
    x1, x2 = xf[..., : r // 2], xf[..., r // 2:]
    return jnp.concatenate([x1 * cos - x2 * sin, x2 * cos + x1 * sin], axis=-1).astype(x.dtype)


def split_cols(a, sizes):
    out, start = [], 0
    for s in sizes:
        out.append(a[..., start:start + s])
        start += s
    return out


def stick_breaking_block(q, k, v, q_pos, k_pos):
    z = jnp.einsum('bqhd,bkhd->bhqk', q, k).astype(jnp.float32) * (SB_HEAD_DIM ** -0.5)
    causal = k_pos[None, :] < q_pos[:, None]
    log_beta = jax.nn.log_sigmoid(z)
    log_rem = jnp.where(causal, log_beta - z, 0.0)
    tail = lax.cumsum(log_rem, axis=3, reverse=True) - log_rem
    w = jnp.where(causal, jnp.exp(log_beta + tail), 0.0)
    return jnp.einsum('bhqk,bkhd->bqhd', w.astype(v.dtype), v)


def mla_block(q_nope, q_rope, k_nope, k_rope, v, q_pos, k_pos):
    s = (jnp.einsum('bqhn,bkhn->bhqk', q_nope, k_nope)
         + jnp.einsum('bqhr,bkr->bhqk', q_rope, k_rope)).astype(jnp.float32)
    s = s * ((MLA_NOPE_DIM + MLA_ROPE_DIM) ** -0.5)
    visible = (k_pos[None, :] // CHUNK) <= (q_pos[:, None] // CHUNK)
    s = jnp.where(visible, s, jnp.finfo(jnp.float32).min)
    p = jax.nn.softmax(s, axis=-1)
    return jnp.einsum('bhqk,bkhv->bqhv', p.astype(v.dtype), v)


def sweep_query_blocks(fn, qs, kvs, q_pos, k_pos):
    b, t = qs[0].shape[0], qs[0].shape[1]
    if t <= Q_BLOCK:
        return fn(*qs, *kvs, q_pos, k_pos)
    n = t // Q_BLOCK
    qs_b = tuple(jnp.moveaxis(a.reshape(b, n, Q_BLOCK, *a.shape[2:]), 1, 0) for a in qs)

    def one(args):
        blk, pos = args
        return fn(*blk, *kvs, pos, k_pos)

    out = lax.map(one, (qs_b, q_pos.reshape(n, Q_BLOCK)))
    return jnp.moveaxis(out, 0, 1).reshape(b, t, *out.shape[3:])


def mixer_layer(x, c, pos, past_sb_k, past_sb_v, past_ckv, past_krope,
                ada_w, ada_b, pre_g, w_in, q_norm_g, w_uq, kv_norm_g, w_ukv, w_out, post_g):
    b, t, _ = x.shape
    mod = jax.nn.silu(c) @ ada_w + ada_b
    shift, scale, gate = jnp.split(mod[:, None, :], 3, axis=-1)
    h = rms_norm(x, pre_g) * (1.0 + scale) + shift
    proj = h @ w_in
    sb_q, sb_k, sb_v, sb_g, c_q, c_kv, k_rope, mla_g = split_cols(proj, IN_SIZES)
    sb_q = sb_q.reshape(b, t, SB_HEADS, SB_HEAD_DIM)
    sb_k = sb_k.reshape(b, t, SB_HEADS, SB_HEAD_DIM)
    sb_v = sb_v.reshape(b, t, SB_HEADS, SB_HEAD_DIM)
    q = (rms_norm(c_q, q_norm_g) @ w_uq).reshape(b, t, MLA_HEADS, MLA_NOPE_DIM + MLA_ROPE_DIM)
    q_nope = q[..., :MLA_NOPE_DIM]
    q_rope = rope(q[..., MLA_NOPE_DIM:], pos)
    c_kv = rms_norm(c_kv, kv_norm_g)
    k_rope = rope(k_rope[:, :, None, :], pos)[:, :, 0, :]
    if past_sb_k is None:
        all_sb_k, all_sb_v, all_ckv, all_krope, k_pos = sb_k, sb_v, c_kv, k_rope, pos
    else:
        past_len = past_sb_k.shape[1]
        all_sb_k = jnp.concatenate([past_sb_k, sb_k], axis=1)
        all_sb_v = jnp.concatenate([past_sb_v, sb_v], axis=1)
        all_ckv = jnp.concatenate([past_ckv, c_kv], axis=1)
        all_krope = jnp.concatenate([past_krope, k_rope], axis=1)
        k_pos = jnp.concatenate([jnp.arange(past_len, dtype=jnp.int32), pos])
    tk = all_ckv.shape[1]
    kv = (all_ckv @ w_ukv).reshape(b, tk, MLA_HEADS, MLA_NOPE_DIM + MLA_V_DIM)
    k_nope, mla_v = kv[..., :MLA_NOPE_DIM], kv[..., MLA_NOPE_DIM:]
    sb_o = sweep_query_blocks(stick_breaking_block, (sb_q,), (all_sb_k, all_sb_v), pos, k_pos)
    mla_o = sweep_query_blocks(mla_block, (q_nope, q_rope), (k_nope, all_krope, mla_v), pos, k_pos)
    mixed = jnp.concatenate([sb_o.reshape(b, t, SB_WIDTH) * jax.nn.silu(sb_g),
                             mla_o.reshape(b, t, MLA_WIDTH) * jax.nn.silu(mla_g)], axis=-1)
    y = x + gate * rms_norm(mixed @ w_out, post_g)
    return y, sb_k, sb_v, c_kv, k_rope


def setup_inputs(seed: int = 0) -> dict:
    key = jax.random.key(seed)
    ks = jax.random.split(key, 20)
    f32 = jnp.float32
    nrm = lambda k, shape, s=1.0: (jax.random.normal(k, shape, f32) * s)
    return {
        "x_prompt": nrm(ks[0], (BATCH, SEQ, D_MODEL)),
        "x_sample": nrm(ks[1], (DEC_BATCH, DEC_SEQ, D_MODEL)),
        "cache_sb_k": nrm(ks[2], (DEPTH, DEC_BATCH, PAST_LEN, SB_HEADS, SB_HEAD_DIM)),
        "cache_sb_v": nrm(ks[3], (DEPTH, DEC_BATCH, PAST_LEN, SB_HEADS, SB_HEAD_DIM)),
        "cache_mla_ckv": nrm(ks[4], (DEPTH, DEC_BATCH, PAST_LEN, KV_LORA_RANK)),
        "cache_mla_krope": nrm(ks[5], (DEPTH, DEC_BATCH, PAST_LEN, MLA_ROPE_DIM)),
        "c_prompt": nrm(ks[6], (BATCH, D_MODEL)),
        "c_sample": nrm(ks[7], (DEC_BATCH, D_MODEL)),
        "ada_w": nrm(ks[8], (DEPTH, D_MODEL, 3 * D_MODEL), D_MODEL ** -0.5),
        "ada_b": nrm(ks[9], (DEPTH, 3 * D_MODEL), 0.01),
        "pre_norm_g": 1.0 + nrm(ks[10], (DEPTH, D_MODEL), 0.01),
        "w_in": nrm(ks[11], (DEPTH, D_MODEL, IN_COLS), D_MODEL ** -0.5),
        "q_norm_g": 1.0 + nrm(ks[12], (DEPTH, Q_LORA_RANK), 0.01),
        "w_uq": nrm(ks[13], (DEPTH, Q_LORA_RANK, MLA_HEADS * (MLA_NOPE_DIM + MLA_ROPE_DIM)), Q_LORA_RANK ** -0.5),
        "kv_norm_g": 1.0 + nrm(ks[14], (DEPTH, KV_LORA_RANK), 0.01),
        "w_ukv": nrm(ks[15], (DEPTH, KV_LORA_RANK, MLA_HEADS * (MLA_NOPE_DIM + MLA_V_DIM)), KV_LORA_RANK ** -0.5),
        "w_out": nrm(ks[16], (DEPTH, D_MIX, D_MODEL), D_MIX ** -0.5),
        "post_norm_g": 1.0 + nrm(ks[17], (DEPTH, D_MODEL), 0.01),
    }


def reference(x_prompt, x_sample, cache_sb_k, cache_sb_v, cache_mla_ckv, cache_mla_krope,
              c_prompt, c_sample, ada_w, ada_b, pre_norm_g, w_in, q_norm_g, w_uq,
              kv_norm_g, w_ukv, w_out, post_norm_g):
    pos_p = jnp.arange(x_prompt.shape[1], dtype=jnp.int32)
    pos_s = cache_sb_k.shape[2] + jnp.arange(x_sample.shape[1], dtype=jnp.int32)
    yp, ys = x_prompt, x_sample
    sk_p, sv_p, ckv_p, kr_p = [], [], [], []
    sk_s, sv_s, ckv_s, kr_s = [], [], [], []
    for l in range(DEPTH):
        w = (ada_w[l], ada_b[l], pre_norm_g[l], w_in[l], q_norm_g[l], w_uq[l],
             kv_norm_g[l], w_ukv[l], w_out[l], post_norm_g[l])
        yp, a, b_, c_, d_ = mixer_layer(yp, c_prompt, pos_p, None, None, None, None, *w)
        sk_p.append(a); sv_p.append(b_); ckv_p.append(c_); kr_p.append(d_)
        ys, a, b_, c_, d_ = mixer_layer(ys, c_sample, pos_s, cache_sb_k[l], cache_sb_v[l],
                                        cache_mla_ckv[l], cache_mla_krope[l], *w)
        sk_s.append(a); sv_s.append(b_); ckv_s.append(c_); kr_s.append(d_)
    return (yp, ys,
            jnp.stack(sk_p), jnp.stack(sv_p), jnp.stack(ckv_p), jnp.stack(kr_p),
            jnp.stack(sk_s), jnp.stack(sv_s), jnp.stack(ckv_s), jnp.stack(kr_s))
```

```python
import functools

import jax
import jax.numpy as jnp
from jax import lax
from jax.experimental import pallas as pl
from jax.experimental.pallas import tpu as pltpu

F32 = jnp.float32
BF16 = jnp.bfloat16

CHUNK = 64
SB_HEAD_DIM = 64
MLA_V_DIM = 64
MLA_NOPE_DIM = 64
MLA_ROPE_DIM = 32
ROPE_THETA = 10000.0
EPS = 1e-6
LANES = 128
MASKED_SCORE = -1e30

VMEM_LIMIT = 56 * 1024 * 1024


def _cparams(n_grid):
    return pltpu.CompilerParams(dimension_semantics=("arbitrary",) * n_grid,
                                vmem_limit_bytes=VMEM_LIMIT)


def _silu(x):
    return x * (1.0 / (1.0 + jnp.exp(-x)))


def _rms(x, g):
    return x * lax.rsqrt(jnp.mean(x * x, axis=-1, keepdims=True) + EPS) * g


def _dot(a, b):
    return jnp.dot(a, b, preferred_element_type=F32)


def _dot_nt(a, b):
    return lax.dot_general(a, b, (((1,), (1,)), ((), ())), preferred_element_type=F32)


def _mod_kernel(c_ref, w_ref, b_ref, o_ref):
    s = _silu(c_ref[...]).astype(BF16)
    o_ref[...] = _dot(s, w_ref[...]) + b_ref[...]


def _mod_call(c_all, ada_w16, ada_b):
    n, d3 = c_all.shape[0], ada_w16.shape[1]
    return pl.pallas_call(
        _mod_kernel,
        out_shape=jax.ShapeDtypeStruct((n, d3), F32),
        compiler_params=pltpu.CompilerParams(vmem_limit_bytes=VMEM_LIMIT),
        name="ada_mod",
    )(c_all, ada_w16, ada_b)


_C_SBQ, _C_SBK, _C_SBV, _C_SBG = 0, 512, 1024, 1536
_C_CQ, _C_CKV, _C_MLAG, _C_KR, _C_END = 2048, 2432, 2688, 3200, 3456


def _proj_kernel(x_ref, mod_ref, pg_ref, w_ref, qg_ref, wab_ref, kvg_ref, wkn_ref, wv_ref, p_ref,
                 cq_ref, sq_ref, ck_ref, sk_ref,
                 sbq_ref, sbk32_ref, sbk16_ref, sbv32_ref, sbv16_ref, gs_ref, qcat_ref,
                 ckv_ref, kcat_ref, mv_ref, kr_ref):
    x = x_ref[0]
    rs = lax.rsqrt(jnp.mean(x * x, axis=-1, keepdims=True) + EPS)
    g = pg_ref[...] * (1.0 + mod_ref[0, 1:2, :])
    hb = ((x * rs) * g + mod_ref[0, 0:1, :]).astype(BF16)

    def proj(a, b):
        return _dot(hb, w_ref[:, a:b])

    sbq_ref[0] = (proj(_C_SBQ, _C_SBK) * (SB_HEAD_DIM ** -0.5)).astype(BF16)
    k = proj(_C_SBK, _C_SBV)
    sbk32_ref[0] = k
    sbk16_ref[0] = k.astype(BF16)
    v = proj(_C_SBV, _C_SBG)
    sbv32_ref[0] = v
    sbv16_ref[0] = v.astype(BF16)
    gs_ref[0, :, 0:512] = _silu(proj(_C_SBG, _C_CQ)).astype(BF16)
    gs_ref[0, :, 512:1024] = _silu(proj(_C_MLAG, _C_KR)).astype(BF16)

    cqn = _rms(proj(_C_CQ, _C_CKV), qg_ref[...]).astype(BF16)
    ab = _dot(cqn, wab_ref[...])
    cq_t, sq_t = cq_ref[...], sq_ref[...]
    n_heads = qcat_ref.shape[2] // LANES
    for h in range(n_heads):
        a = ab[:, h * LANES:(h + 1) * LANES]
        b = ab[:, (n_heads + h) * LANES:(n_heads + h + 1) * LANES]
        qcat_ref[0, :, h * LANES:(h + 1) * LANES] = (a * cq_t + b * sq_t).astype(BF16)

    ckvn = _rms(proj(_C_CKV, _C_MLAG), kvg_ref[...])
    ckv_ref[0] = ckvn
    cb = ckvn.astype(BF16)
    r = proj(_C_KR, _C_END)
    kro = r[:, :LANES] * ck_ref[...] + r[:, LANES:] * sk_ref[...]
    kr_ref[0] = kro[:, :MLA_ROPE_DIM]
    kcat_ref[0] = (_dot(cb, wkn_ref[...]) + _dot(kro.astype(BF16), p_ref[...])).astype(BF16)
    mv_ref[0] = _dot(cb, wv_ref[...]).astype(BF16)


def _proj_call(x, mod, pw, tabs, tm):
    b, t, d = x.shape
    grid = (b, t // tm)
    row = lambda n: pl.BlockSpec((1, tm, n), lambda i, j: (i, j, 0))
    full = lambda a: pl.BlockSpec(a.shape, lambda i, j: (0,) * a.ndim)
    tab = pl.BlockSpec((tm, LANES), lambda i, j: (j, 0))
    o = lambda n, dt: jax.ShapeDtypeStruct((b, t, n), dt)
    return pl.pallas_call(
        _proj_kernel,
        grid=grid,
        in_specs=[row(d), pl.BlockSpec((1, 3, d), lambda i, j: (i, 0, 0)), full(pw["pre_g"]),
                  full(pw["w_aug"]), full(pw["q_g"]), full(pw["w_ab"]), full(pw["kv_g"]),
                  full(pw["w_kn"]), full(pw["w_v"]), full(pw["place"]), tab, tab, tab, tab],
        out_specs=[row(512), row(512), row(512), row(512), row(512), row(1024), row(1024),
                   row(256), row(1024), row(512), row(MLA_ROPE_DIM)],
        out_shape=[o(512, BF16), o(512, F32), o(512, BF16), o(512, F32), o(512, BF16),
                   o(1024, BF16), o(1024, BF16), o(256, F32), o(1024, BF16), o(512, BF16),
                   o(MLA_ROPE_DIM, F32)],
        compiler_params=_cparams(2),
        name="in_proj",
    )(x, mod, pw["pre_g"], pw["w_aug"], pw["q_g"], pw["w_ab"], pw["kv_g"], pw["w_kn"], pw["w_v"],
      pw["place"], tabs["cq"], tabs["sq"], tabs["ck"], tabs["sk"])


def _sb_block(qh, k, v, u, causal, r_sum, acc):
    z = _dot_nt(qh, k)
    l = jnp.log(1.0 + jnp.exp(-jnp.abs(z)))
    lb = jnp.minimum(z, 0.0) - l
    lr = lb - z
    if causal is not None:
        lr = jnp.where(causal, lr, 0.0)
    hi = lr.astype(BF16)
    lo = (lr - hi.astype(F32)).astype(BF16)
    tail = _dot(hi, u) + _dot(lo, u)
    w = jnp.exp(lb + tail + r_sum)
    if causal is not None:
        w = jnp.where(causal, w, 0.0)
    acc = acc + _dot(w.astype(BF16), v)
    r_sum = r_sum + jnp.sum(lr, axis=-1, keepdims=True)
    return r_sum, acc


def _split_heads(qp):
    lane = lax.broadcasted_iota(jnp.int32, qp.shape, 1)
    qf = qp.astype(F32)
    first = lane < SB_HEAD_DIM
    return (jnp.where(first, qf, 0.0).astype(BF16), jnp.where(first, 0.0, qf).astype(BF16), first)


def _sb_prompt_kernel(q_ref, k_ref, v_ref, u_ref, o_ref, *, tq):
    qi = pl.program_id(2)
    q0, q1, first = _split_heads(q_ref[0])
    row = lax.broadcasted_iota(jnp.int32, (tq, tq), 0)
    col = lax.broadcasted_iota(jnp.int32, (tq, tq), 1)
    causal = col < row
    u = u_ref[...]
    zr = jnp.zeros((tq, 1), F32)
    za = jnp.zeros((tq, LANES), F32)

    off = pl.multiple_of(qi * tq, tq)
    kd = k_ref[0, pl.ds(off, tq), :]
    vd = v_ref[0, pl.ds(off, tq), :]
    r0, a0 = _sb_block(q0, kd, vd, u, causal, zr, za)
    r1, a1 = _sb_block(q1, kd, vd, u, causal, zr, za)

    def body(i, c):
        r0, a0, r1, a1 = c
        o = pl.multiple_of((qi - 1 - i) * tq, tq)
        k = k_ref[0, pl.ds(o, tq), :]
        v = v_ref[0, pl.ds(o, tq), :]
        r0, a0 = _sb_block(q0, k, v, u, None, r0, a0)
        r1, a1 = _sb_block(q1, k, v, u, None, r1, a1)
        return r0, a0, r1, a1

    r0, a0, r1, a1 = lax.fori_loop(0, qi, body, (r0, a0, r1, a1))
    o_ref[0] = jnp.where(first, a0, a1).astype(BF16)


def _suffix_matrix(n):
    j = lax.broadcasted_iota(jnp.int32, (n, n), 0)
    s = lax.broadcasted_iota(jnp.int32, (n, n), 1)
    return (j > s).astype(BF16)


def _sb_prompt_call(q, k, v, tq):
    b, t, w = q.shape
    grid = (b, w // LANES, t // tq)
    return pl.pallas_call(
        functools.partial(_sb_prompt_kernel, tq=tq),
        grid=grid,
        in_specs=[pl.BlockSpec((1, tq, LANES), lambda i, h, j: (i, j, h)),
                  pl.BlockSpec((1, t, LANES), lambda i, h, j: (i, 0, h)),
                  pl.BlockSpec((1, t, LANES), lambda i, h, j: (i, 0, h)),
                  pl.BlockSpec((tq, tq), lambda i, h, j: (0, 0))],
        out_specs=pl.BlockSpec((1, tq, LANES), lambda i, h, j: (i, j, h)),
        out_shape=jax.ShapeDtypeStruct((b, t, w), BF16),
        compiler_params=_cparams(3),
        name="sb_prompt",
    )(q, k, v, _suffix_matrix(tq))


def _sb_sample_kernel(q_ref, kn_ref, vn_ref, kp_ref, vp_ref, us_ref, u_ref, o_ref, *, tk):
    tq = q_ref.shape[1]
    n_past = kp_ref.shape[1] // tk
    q0, q1, first = _split_heads(q_ref[0])
    row = lax.broadcasted_iota(jnp.int32, (tq, tq), 0)
    col = lax.broadcasted_iota(jnp.int32, (tq, tq), 1)
    causal = col < row
    u = u_ref[...]
    zr = jnp.zeros((tq, 1), F32)
    za = jnp.zeros((tq, LANES), F32)
    r0, a0 = _sb_block(q0, kn_ref[0], vn_ref[0], us_ref[...], causal, zr, za)
    r1, a1 = _sb_block(q1, kn_ref[0], vn_ref[0], us_ref[...], causal, zr, za)

    def body(i, c):
        r0, a0, r1, a1 = c
        o = pl.multiple_of((n_past - 1 - i) * tk, tk)
        k = kp_ref[0, pl.ds(o, tk), :].astype(BF16)
        v = vp_ref[0, pl.ds(o, tk), :].astype(BF16)
        r0, a0 = _sb_block(q0, k, v, u, None, r0, a0)
        r1, a1 = _sb_block(q1, k, v, u, None, r1, a1)
        return r0, a0, r1, a1

    r0, a0, r1, a1 = lax.fori_loop(0, n_past, body, (r0, a0, r1, a1))
    o_ref[0] = jnp.where(first, a0, a1).astype(BF16)


def _sb_sample_call(q, kn, vn, kp, vp, tk):
    b, t, w = q.shape
    tp = kp.shape[1]
    new = pl.BlockSpec((1, t, LANES), lambda i, h: (i, 0, h))
    past = pl.BlockSpec((1, tp, LANES), lambda i, h: (i, 0, h))
    return pl.pallas_call(
        functools.partial(_sb_sample_kernel, tk=tk),
        grid=(b, w // LANES),
        in_specs=[new, new, new, past, past,
                  pl.BlockSpec((t, t), lambda i, h: (0, 0)),
                  pl.BlockSpec((tk, tk), lambda i, h: (0, 0))],
        out_specs=new,
        out_shape=jax.ShapeDtypeStruct((b, t, w), BF16),
        compiler_params=_cparams(2),
        name="sb_sample",
    )(q, kn, vn, kp, vp, _suffix_matrix(t), _suffix_matrix(tk))


def _mla_block(qh, k, v, mask, m, l, acc):
    s = _dot_nt(qh, k)
    if mask is not None:
        s = jnp.where(mask, s, MASKED_SCORE)
    m_new = jnp.maximum(m, jnp.max(s, axis=-1, keepdims=True))
    p = jnp.exp(s - m_new)
    alpha = jnp.exp(m - m_new)
    l = alpha * l + jnp.sum(p, axis=-1, keepdims=True)
    acc = alpha * acc + _dot(p.astype(BF16), v)
    return m_new, l, acc


def _mla_init(tq):
    return (jnp.full((tq, 1), MASKED_SCORE, F32), jnp.zeros((tq, 1), F32),
            jnp.zeros((tq, LANES), F32))


def _mla_finish(s0, s1, o_ref):
    lane = lax.broadcasted_iota(jnp.int32, s0[2].shape, 1)
    o0 = s0[2] * (1.0 / s0[1])
    o1 = s1[2] * (1.0 / s1[1])
    o_ref[0] = jnp.where(lane < MLA_V_DIM, o0, o1).astype(BF16)


def _mla_prompt_kernel(q_ref, k_ref, v_ref, o_ref, *, tq):
    qi = pl.program_id(2)
    q0 = q_ref[0, :, :LANES]
    q1 = q_ref[0, :, LANES:]
    row = lax.broadcasted_iota(jnp.int32, (tq, tq), 0)
    col = lax.broadcasted_iota(jnp.int32, (tq, tq), 1)
    visible = (col // CHUNK) <= (row // CHUNK)

    off = pl.multiple_of(qi * tq, tq)
    kd = k_ref[0, pl.ds(off, tq), :]
    vd = v_ref[0, pl.ds(off, tq), :]
    s0 = _mla_block(q0, kd[:, :LANES], vd, visible, *_mla_init(tq))
    s1 = _mla_block(q1, kd[:, LANES:], vd, visible, *_mla_init(tq))

    def body(i, c):
        s0, s1 = c
        o = pl.multiple_of(i * tq, tq)
        k = k_ref[0, pl.ds(o, tq), :]
        v = v_ref[0, pl.ds(o, tq), :]
        s0 = _mla_block(q0, k[:, :LANES], v, None, *s0)
        s1 = _mla_block(q1, k[:, LANES:], v, None, *s1)
        return s0, s1

    s0, s1 = lax.fori_loop(0, qi, body, (s0, s1))
    _mla_finish(s0, s1, o_ref)


def _mla_prompt_call(q, k, v, tq):
    b, t, w = v.shape
    grid = (b, w // LANES, t // tq)
    return pl.pallas_call(
        functools.partial(_mla_prompt_kernel, tq=tq),
        grid=grid,
        in_specs=[pl.BlockSpec((1, tq, 2 * LANES), lambda i, h, j: (i, j, h)),
                  pl.BlockSpec((1, t, 2 * LANES), lambda i, h, j: (i, 0, h)),
                  pl.BlockSpec((1, t, LANES), lambda i, h, j: (i, 0, h))],
        out_specs=pl.BlockSpec((1, tq, LANES), lambda i, h, j: (i, j, h)),
        out_shape=jax.ShapeDtypeStruct((b, t, w), BF16),
        compiler_params=_cparams(3),
        name="mla_prompt",
    )(q, k, v)


def _mla_sample_kernel(q_ref, kn_ref, vn_ref, kp_ref, vp_ref, o_ref, *, tk, new_mask):
    tq = q_ref.shape[1]
    n_past = kp_ref.shape[1] // tk
    q0 = q_ref[0, :, :LANES]
    q1 = q_ref[0, :, LANES:]
    mask = None
    if new_mask is not None:
        qc, kc = new_mask
        row = lax.broadcasted_iota(jnp.int32, (tq, tq), 0) + qc
        col = lax.broadcasted_iota(jnp.int32, (tq, tq), 1) + kc
        mask = (col // CHUNK) <= (row // CHUNK)
    kn = kn_ref[0]
    s0 = _mla_block(q0, kn[:, :LANES], vn_ref[0], mask, *_mla_init(tq))
    s1 = _mla_block(q1, kn[:, LANES:], vn_ref[0], mask, *_mla_init(tq))

    def body(i, c):
        s0, s1 = c
        o = pl.multiple_of(i * tk, tk)
        k = kp_ref[0, pl.ds(o, tk), :]
        v = vp_ref[0, pl.ds(o, tk), :]
        s0 = _mla_block(q0, k[:, :LANES], v, None, *s0)
        s1 = _mla_block(q1, k[:, LANES:], v, None, *s1)
        return s0, s1

    s0, s1 = lax.fori_loop(0, n_past, body, (s0, s1))
    _mla_finish(s0, s1, o_ref)


def _mla_sample_call(q, kn, vn, kp, vp, tk, past_len):
    b, t, w = vn.shape
    tp = kp.shape[1]
    assert past_len % CHUNK == 0
    new_mask = None if t <= CHUNK else (past_len, past_len)
    return pl.pallas_call(
        functools.partial(_mla_sample_kernel, tk=tk, new_mask=new_mask),
        grid=(b, w // LANES),
        in_specs=[pl.BlockSpec((1, t, 2 * LANES), lambda i, h: (i, 0, h)),
                  pl.BlockSpec((1, t, 2 * LANES), lambda i, h: (i, 0, h)),
                  pl.BlockSpec((1, t, LANES), lambda i, h: (i, 0, h)),
                  pl.BlockSpec((1, tp, 2 * LANES), lambda i, h: (i, 0, h)),
                  pl.BlockSpec((1, tp, LANES), lambda i, h: (i, 0, h))],
        out_specs=pl.BlockSpec((1, t, LANES), lambda i, h: (i, 0, h)),
        out_shape=jax.ShapeDtypeStruct((b, t, w), BF16),
        compiler_params=_cparams(2),
        name="mla_sample",
    )(q, kn, vn, kp, vp)


def _past_kv_kernel(ckv_ref, kr_ref, wkn_ref, wv_ref, p_ref, kcat_ref, mv_ref):
    cb = ckv_ref[0].astype(BF16)
    kr = kr_ref[0].astype(BF16)
    kcat_ref[0] = (_dot(cb, wkn_ref[...]) + _dot(kr, p_ref[...])).astype(BF16)
    mv_ref[0] = _dot(cb, wv_ref[...]).astype(BF16)


def _past_kv_call(ckv, kr, pw, tm):
    b, t, r = ckv.shape
    full = lambda a: pl.BlockSpec(a.shape, lambda i, j: (0,) * a.ndim)
    row = lambda n: pl.BlockSpec((1, tm, n), lambda i, j: (i, j, 0))
    place = pw["place"][:MLA_ROPE_DIM]
    return pl.pallas_call(
        _past_kv_kernel,
        grid=(b, t // tm),
        in_specs=[row(r), row(MLA_ROPE_DIM), full(pw["w_kn"]), full(pw["w_v"]), full(place)],
        out_specs=[row(1024), row(512)],
        out_shape=[jax.ShapeDtypeStruct((b, t, 1024), BF16), jax.ShapeDtypeStruct((b, t, 512), BF16)],
        compiler_params=_cparams(2),
        name="past_kv",
    )(ckv, kr, pw["w_kn"], pw["w_v"], place)


def _out_kernel(sbo_ref, mlao_ref, gs_ref, x_ref, mod_ref, w_ref, g_ref, y_ref):
    half = sbo_ref.shape[2]
    m0 = (sbo_ref[0].astype(F32) * gs_ref[0, :, :half].astype(F32)).astype(BF16)
    m1 = (mlao_ref[0].astype(F32) * gs_ref[0, :, half:].astype(F32)).astype(BF16)
    out = _dot(m0, w_ref[:half, :]) + _dot(m1, w_ref[half:, :])
    y_ref[0] = x_ref[0] + mod_ref[0, 2:3, :] * _rms(out, g_ref[...])


def _out_call(sbo, mlao, gs, x, mod, w_out16, post_g, tm):
    b, t, d = x.shape
    row = lambda n: pl.BlockSpec((1, tm, n), lambda i, j: (i, j, 0))
    full = lambda a: pl.BlockSpec(a.shape, lambda i, j: (0,) * a.ndim)
    return pl.pallas_call(
        _out_kernel,
        grid=(b, t // tm),
        in_specs=[row(512), row(512), row(1024), row(d),
                  pl.BlockSpec((1, 3, d), lambda i, j: (i, 0, 0)), full(w_out16), full(post_g)],
        out_specs=row(d),
        out_shape=jax.ShapeDtypeStruct((b, t, d), F32),
        compiler_params=_cparams(2),
        name="out_proj",
    )(sbo, mlao, gs, x, mod, w_out16, post_g)


def _prep_weights(pre_g, w_in, q_g, w_uq, kv_g, w_ukv):
    d = w_in.shape[0]
    n_heads = w_ukv.shape[1] // (MLA_NOPE_DIM + MLA_V_DIM)
    half = MLA_ROPE_DIM // 2
    sizes = (512, 512, 512, 512, w_uq.shape[0], w_ukv.shape[0], MLA_ROPE_DIM, 512)
    starts = [0]
    for s in sizes:
        starts.append(starts[-1] + s)
    cols = lambda i: w_in[:, starts[i]:starts[i + 1]]
    kr = cols(6)
    kr_swapped = jnp.concatenate([-kr[:, half:], kr[:, :half]], axis=1)
    zpad = jnp.zeros((d, LANES - MLA_ROPE_DIM), w_in.dtype)
    w_aug = jnp.concatenate([cols(0), cols(1), cols(2), cols(3), cols(4), cols(5), cols(7),
                             kr, zpad, kr_swapped, zpad], axis=1).astype(BF16)
    assert w_aug.shape[1] == _C_END

    r_q = w_uq.shape[0]
    uq = w_uq.reshape(r_q, n_heads, MLA_NOPE_DIM + MLA_ROPE_DIM)
    nope, rope = uq[..., :MLA_NOPE_DIM], uq[..., MLA_NOPE_DIM:]
    zq = jnp.zeros((r_q, n_heads, LANES - MLA_NOPE_DIM - MLA_ROPE_DIM), w_uq.dtype)
    w_a = jnp.concatenate([nope, rope, zq], axis=-1).reshape(r_q, n_heads * LANES)
    rope_swapped = jnp.concatenate([-rope[..., half:], rope[..., :half]], axis=-1)
    w_b = jnp.concatenate([jnp.zeros_like(nope), rope_swapped, zq], axis=-1).reshape(r_q, n_heads * LANES)
    w_ab = jnp.concatenate([w_a, w_b], axis=1).astype(BF16)

    r_kv = w_ukv.shape[0]
    ukv = w_ukv.reshape(r_kv, n_heads, MLA_NOPE_DIM + MLA_V_DIM)
    w_kn = jnp.concatenate([ukv[..., :MLA_NOPE_DIM],
                            jnp.zeros((r_kv, n_heads, LANES - MLA_NOPE_DIM), w_ukv.dtype)],
                           axis=-1).reshape(r_kv, n_heads * LANES).astype(BF16)
    w_v = ukv[..., MLA_NOPE_DIM:].reshape(r_kv, n_heads * MLA_V_DIM).astype(BF16)

    src = lax.broadcasted_iota(jnp.int32, (LANES, n_heads * LANES), 0)
    dst = lax.broadcasted_iota(jnp.int32, (LANES, n_heads * LANES), 1)
    place = ((dst % LANES == src + MLA_NOPE_DIM) & (src < MLA_ROPE_DIM)).astype(BF16)

    return dict(pre_g=pre_g[None, :], w_aug=w_aug, q_g=q_g[None, :], w_ab=w_ab,
                kv_g=kv_g[None, :], w_kn=w_kn, w_v=w_v, place=place)


def _rope_tables(pos):
    r = MLA_ROPE_DIM
    inv_freq = ROPE_THETA ** (-jnp.arange(0, r, 2, dtype=F32) / r)
    ang = pos.astype(F32)[:, None] * inv_freq[None, :]
    cos = jnp.concatenate([jnp.cos(ang), jnp.cos(ang)], axis=1)
    sin = jnp.concatenate([jnp.sin(ang), jnp.sin(ang)], axis=1)
    t = pos.shape[0]
    scale = (MLA_NOPE_DIM + MLA_ROPE_DIM) ** -0.5
    zq = jnp.zeros((t, LANES - MLA_NOPE_DIM - r), F32)
    cq = jnp.concatenate([jnp.full((t, MLA_NOPE_DIM), scale, F32), cos * scale, zq], axis=1)
    sq = jnp.concatenate([jnp.zeros((t, MLA_NOPE_DIM), F32), sin * scale, zq], axis=1)
    zk = jnp.zeros((t, LANES - r), F32)
    ck = jnp.concatenate([cos, zk], axis=1)
    sk = jnp.concatenate([sin, zk], axis=1)
    return dict(cq=cq, sq=sq, ck=ck, sk=sk)


PROMPT_TM = 256
PROMPT_TQ = 256
OUT_TM = 512
PAST_TK = 256


def kernel(x_prompt, x_sample, cache_sb_k, cache_sb_v, cache_mla_ckv, cache_mla_krope, c_prompt, c_sample, ada_w, ada_b, pre_norm_g, w_in, q_norm_g, w_uq, kv_norm_g, w_ukv, w_out, post_norm_g):
    depth = ada_w.shape[0]
    bp, tp, d = x_prompt.shape
    bs, ts, _ = x_sample.shape
    past_len = cache_sb_k.shape[2]
    n_sb = cache_sb_k.shape[3]

    tabs_p = _rope_tables(jnp.arange(tp, dtype=jnp.int32))
    tabs_s = _rope_tables(past_len + jnp.arange(ts, dtype=jnp.int32))
    c_all = jnp.concatenate([c_prompt, c_sample], axis=0)

    yp, ys = x_prompt, x_sample
    new_p, new_s = [], []
    for l in range(depth):
        pw = _prep_weights(pre_norm_g[l], w_in[l], q_norm_g[l], w_uq[l], kv_norm_g[l], w_ukv[l])
        w_out16 = w_out[l].astype(BF16)
        post_g = post_norm_g[l][None, :]
        mod = _mod_call(c_all, ada_w[l].astype(BF16), ada_b[l][None, :])
        mod_p = mod[:bp].reshape(bp, 3, d)
        mod_s = mod[bp:].reshape(bs, 3, d)

        (sbq, sbk32, sbk16, sbv32, sbv16, gs, qcat, ckv, kcat, mv, kr) = _proj_call(
            yp, mod_p, pw, tabs_p, PROMPT_TM)
        sbo = _sb_prompt_call(sbq, sbk16, sbv16, PROMPT_TQ)
        mlao = _mla_prompt_call(qcat, kcat, mv, PROMPT_TQ)
        yp = _out_call(sbo, mlao, gs, yp, mod_p, w_out16, post_g, OUT_TM)
        new_p.append((sbk32.reshape(bp, tp, n_sb, SB_HEAD_DIM), sbv32.reshape(bp, tp, n_sb, SB_HEAD_DIM),
                      ckv, kr))

        (sbq, sbk32, sbk16, sbv32, sbv16, gs, qcat, ckv, kcat, mv, kr) = _proj_call(
            ys, mod_s, pw, tabs_s, ts)
        kp = cache_sb_k[l].reshape(bs, past_len, n_sb * SB_HEAD_DIM)
        vp = cache_sb_v[l].reshape(bs, past_len, n_sb * SB_HEAD_DIM)
        sbo = _sb_sample_call(sbq, sbk16, sbv16, kp, vp, PAST_TK)
        kcat_p, mv_p = _past_kv_call(cache_mla_ckv[l], cache_mla_krope[l], pw, 512)
        mlao = _mla_sample_call(qcat, kcat, mv, kcat_p, mv_p, PAST_TK, past_len)
        ys = _out_call(sbo, mlao, gs, ys, mod_s, w_out16, post_g, ts)
        new_s.append((sbk32.reshape(bs, ts, n_sb, SB_HEAD_DIM), sbv32.reshape(bs, ts, n_sb, SB_HEAD_DIM),
                      ckv, kr))

    stack = lambda items, i: items[0][i][None] if depth == 1 else jnp.stack([it[i] for it in items])
    return (yp, ys,
            stack(new_p, 0), stack(new_p, 1), stack(new_p, 2), stack(new_p, 3),
            stack(new_s, 0), stack(new_s, 1), stack(new_s, 2), stack(new_s, 3))
```

```python
import functools

import jax
import jax.numpy as jnp
from jax import lax
from jax.experimental import pallas as pl
from jax.experimental.pallas import tpu as pltpu

F32 = jnp.float32
BF16 = jnp.bfloat16

CHUNK = 64
SB_HEAD_DIM = 64
MLA_V_DIM = 64
MLA_NOPE_DIM = 64
MLA_ROPE_DIM = 32
ROPE_THETA = 10000.0
EPS = 1e-6
LOG2E = 1.4426950408889634
LANES = 128
HEAD = 64
MASKED_SCORE = -1e30

VMEM_LIMIT = 56 * 1024 * 1024


def _cparams(n_grid, flags=None):
    return pltpu.CompilerParams(dimension_semantics=("arbitrary",) * n_grid,
                                vmem_limit_bytes=VMEM_LIMIT, flags=flags)


def _silu(x):
    return x * (1.0 / (1.0 + jnp.exp(-x)))


def _rms(x, g):
    return x * lax.rsqrt(jnp.mean(x * x, axis=-1, keepdims=True) + EPS) * g


def _dot(a, b):
    return jnp.dot(a, b, preferred_element_type=F32)


def _dot_nt(a, b):
    return lax.dot_general(a, b, (((1,), (1,)), ((), ())), preferred_element_type=F32)


def _mod_kernel(c_ref, w_ref, b_ref, o_ref):
    s = _silu(c_ref[...]).astype(BF16)
    o_ref[...] = _dot(s, w_ref[...]) + b_ref[...]


def _mod_call(c_all, ada_w16, ada_b):
    n, d3 = c_all.shape[0], ada_w16.shape[1]
    return pl.pallas_call(
        _mod_kernel,
        out_shape=jax.ShapeDtypeStruct((n, d3), F32),
        compiler_params=pltpu.CompilerParams(vmem_limit_bytes=VMEM_LIMIT),
        name="ada_mod",
    )(c_all, ada_w16, ada_b)


_C_SBQ, _C_SBK, _C_SBV, _C_SBG = 0, 512, 1024, 1536
_C_CQ, _C_CKV, _C_MLAG, _C_KR, _C_END = 2048, 2432, 2688, 3200, 3456


def _proj_kernel(x_ref, mod_ref, pg_ref, w_ref, qg_ref, wab_ref, kvg_ref, wkn_ref, wv_ref, p_ref,
                 cq_ref, sq_ref, ck_ref, sk_ref,
                 sbq_ref, sbk32_ref, sbk16_ref, sbv32_ref, sbvt_ref, gs_ref, qcat_ref,
                 ckv_ref, kcat_ref, mvt_ref, kr_ref):
    x = x_ref[0]
    rs = lax.rsqrt(jnp.mean(x * x, axis=-1, keepdims=True) + EPS)
    g = pg_ref[...] * (1.0 + mod_ref[0, 1:2, :])
    hb = ((x * rs) * g + mod_ref[0, 0:1, :]).astype(BF16)

    def proj(a, b):
        return _dot(hb, w_ref[:, a:b])

    sbq_ref[0] = (proj(_C_SBQ, _C_SBK) * (SB_HEAD_DIM ** -0.5 * LOG2E)).astype(BF16)
    k = proj(_C_SBK, _C_SBV)
    sbk32_ref[0] = k
    sbk16_ref[0] = k.astype(BF16)
    v = proj(_C_SBV, _C_SBG)
    sbv32_ref[0] = v
    sbvt_ref[0, 0] = v.T.astype(BF16)
    gs_ref[0, :, 0:512] = _silu(proj(_C_SBG, _C_CQ)).astype(BF16)
    gs_ref[0, :, 512:1024] = _silu(proj(_C_MLAG, _C_KR)).astype(BF16)

    cqn = _rms(proj(_C_CQ, _C_CKV), qg_ref[...]).astype(BF16)
    ab = _dot(cqn, wab_ref[...])
    cq_t, sq_t = cq_ref[...], sq_ref[...]
    n_heads = qcat_ref.shape[2] // LANES
    for h in range(n_heads):
        a = ab[:, h * LANES:(h + 1) * LANES]
        b = ab[:, (n_heads + h) * LANES:(n_heads + h + 1) * LANES]
        qcat_ref[0, :, h * LANES:(h + 1) * LANES] = (a * cq_t + b * sq_t).astype(BF16)

    ckvn = _rms(proj(_C_CKV, _C_MLAG), kvg_ref[...])
    ckv_ref[0] = ckvn
    cb = ckvn.astype(BF16)
    r = proj(_C_KR, _C_END)
    kro = r[:, :LANES] * ck_ref[...] + r[:, LANES:] * sk_ref[...]
    kr_ref[0] = kro[:, :MLA_ROPE_DIM]
    kcat_ref[0] = (_dot(cb, wkn_ref[...]) + _dot(kro.astype(BF16), p_ref[...])).astype(BF16)
    mvt_ref[0, 0] = _dot(cb, wv_ref[...]).T.astype(BF16)


def _proj_call(x, mod, pw, tabs, tm):
    b, t, d = x.shape
    nt = t // tm
    row = lambda n: pl.BlockSpec((1, tm, n), lambda i, j: (i, j, 0))
    colt = pl.BlockSpec((1, 1, 512, tm), lambda i, j: (i, j, 0, 0))
    full = lambda a: pl.BlockSpec(a.shape, lambda i, j: (0,) * a.ndim)
    tab = pl.BlockSpec((tm, LANES), lambda i, j: (j, 0))
    o = lambda n, dt: jax.ShapeDtypeStruct((b, t, n), dt)
    ot = jax.ShapeDtypeStruct((b, nt, 512, tm), BF16)
    return pl.pallas_call(
        _proj_kernel,
        grid=(b, nt),
        in_specs=[row(d), pl.BlockSpec((1, 3, d), lambda i, j: (i, 0, 0)), full(pw["pre_g"]),
                  full(pw["w_aug"]), full(pw["q_g"]), full(pw["w_ab"]), full(pw["kv_g"]),
                  full(pw["w_kn"]), full(pw["w_v"]), full(pw["place"]), tab, tab, tab, tab],
        out_specs=[row(512), row(512), row(512), row(512), colt, row(1024), row(1024),
                   row(256), row(1024), colt, row(MLA_ROPE_DIM)],
        out_shape=[o(512, BF16), o(512, F32), o(512, BF16), o(512, F32), ot,
                   o(1024, BF16), o(1024, BF16), o(256, F32), o(1024, BF16), ot,
                   o(MLA_ROPE_DIM, F32)],
        compiler_params=_cparams(2),
        name="in_proj",
    )(x, mod, pw["pre_g"], pw["w_aug"], pw["q_g"], pw["w_ab"], pw["kv_g"], pw["w_kn"], pw["w_v"],
      pw["place"], tabs["cq"], tabs["sq"], tabs["ck"], tabs["sk"])


def _emit_pipelined(tasks, lag):
    n_slots = max(j + (len(stages) - 1) * lag for j, stages in enumerate(tasks)) + 1
    for slot in range(n_slots):
        for j, stages in enumerate(tasks):
            s, rem = divmod(slot - j, lag)
            if slot >= j and rem == 0 and s < len(stages):
                stages[s]()


def _sb_task(state, c, get_q, get_k, get_vt, lmat, masked):
    t = {}

    def scores():
        t["z"] = _dot_nt(get_k(), get_q())

    def logs():
        z = t.pop("z")
        if masked is not None:
            z = jnp.where(masked, z, MASKED_SCORE)
        low = jnp.minimum(z, 0.0)
        neg_relu = low - z
        l = jnp.log2(1.0 + jnp.exp2(low + neg_relu))
        lb = low - l
        lr = neg_relu - l
        t["tail"] = _dot(lmat, lr.astype(BF16))
        t["e"] = lb + state["r"][c]
        state["r"][c] = state["r"][c] + jnp.sum(lr, axis=0, keepdims=True)

    def weights():
        w = jnp.exp2(t.pop("e") + t.pop("tail"))
        t["pv"] = _dot(get_vt(), w.astype(BF16))

    def accumulate():
        state["acc"][c] = state["acc"][c] + t.pop("pv")

    return [scores, logs, weights, accumulate]


def _split_heads(qp):
    lane = lax.broadcasted_iota(jnp.int32, qp.shape, 1)
    qf = qp.astype(F32)
    first = lane < HEAD
    return jnp.where(first, qf, 0.0).astype(BF16), jnp.where(first, 0.0, qf).astype(BF16)


def _causal_units(nt):
    return [(i, i - dist) for dist in range(nt) for i in range(dist, nt)]


def _sb_prompt_kernel(q_ref, k_ref, vt_ref, l_ref, o_ref, *, tb):
    nt = q_ref.shape[1] // tb
    key = lax.broadcasted_iota(jnp.int32, (tb, tb), 0)
    qry = lax.broadcasted_iota(jnp.int32, (tb, tb), 1)
    causal = key < qry
    lmat = l_ref[...]
    qs = {}

    def unit(i, j, h):
        def get_q():
            if (i, h) not in qs:
                qs[(i, 0)], qs[(i, 1)] = _split_heads(q_ref[0, i * tb:(i + 1) * tb, :])
            return qs[(i, h)]

        def write_tile():
            pair = jnp.concatenate([state["acc"][2 * i], state["acc"][2 * i + 1]], axis=0)
            o_ref[0, i * tb:(i + 1) * tb, :] = pair.T.astype(BF16)

        get_k = lambda: k_ref[0, j * tb:(j + 1) * tb, :]
        get_vt = lambda: vt_ref[0, j, h * HEAD:(h + 1) * HEAD, :]
        stages = _sb_task(state, 2 * i + h, get_q, get_k, get_vt, lmat, causal if i == j else None)
        return stages + [write_tile] if (j == 0 and h == 1) else stages

    state = {"r": [jnp.zeros((1, tb), F32)] * (2 * nt), "acc": [jnp.zeros((HEAD, tb), F32)] * (2 * nt)}
    _emit_pipelined([unit(i, j, h) for i, j in _causal_units(nt) for h in range(2)], SB_LAG)


def _suffix_matrix(n):
    s = lax.broadcasted_iota(jnp.int32, (n, n), 0)
    j = lax.broadcasted_iota(jnp.int32, (n, n), 1)
    return (j > s).astype(BF16)


def _sb_prompt_call(q, k, vt, tb):
    b, t, w = q.shape
    return pl.pallas_call(
        functools.partial(_sb_prompt_kernel, tb=tb),
        grid=(b, w // LANES),
        in_specs=[pl.BlockSpec((1, t, LANES), lambda i, h: (i, 0, h)),
                  pl.BlockSpec((1, t, LANES), lambda i, h: (i, 0, h)),
                  pl.BlockSpec((1, t // tb, LANES, tb), lambda i, h: (i, 0, h, 0)),
                  pl.BlockSpec((tb, tb), lambda i, h: (0, 0))],
        out_specs=pl.BlockSpec((1, t, LANES), lambda i, h: (i, 0, h)),
        out_shape=jax.ShapeDtypeStruct((b, t, w), BF16),
        compiler_params=_cparams(2),
        name="sb_prompt",
    )(q, k, vt, _suffix_matrix(tb))


def _pair_rows(q_lo, q_hi):
    return jnp.concatenate([q_lo, q_hi], axis=0)


def _pair_out(acc):
    acc_t = acc.T
    lane = lax.broadcasted_iota(jnp.int32, (HEAD, LANES), 1)
    return jnp.where(lane < HEAD, acc_t[:HEAD], acc_t[HEAD:])


def _sb_sample_kernel(q_ref, kn_ref, vtn_ref, kp_ref, vp_ref, ls_ref, l_ref, o_ref, *, tk):
    tq = q_ref.shape[1]
    n_past = kp_ref.shape[1] // tk
    qrows = _pair_rows(*_split_heads(q_ref[0]))
    key = lax.broadcasted_iota(jnp.int32, (tq, 2 * tq), 0)
    qry = lax.broadcasted_iota(jnp.int32, (tq, 2 * tq), 1) % tq
    causal = key < qry
    lmat = l_ref[...]
    state = {"r": [jnp.zeros((1, 2 * tq), F32)], "acc": [jnp.zeros((LANES, 2 * tq), F32)]}
    get_q = lambda: qrows
    tasks = [_sb_task(state, 0, get_q, lambda: kn_ref[0], lambda: vtn_ref[0, 0], ls_ref[...], causal)]
    for kb in reversed(range(n_past)):
        get_k = lambda kb=kb: kp_ref[0, kb * tk:(kb + 1) * tk, :].astype(BF16)
        get_vt = lambda kb=kb: vp_ref[0, kb * tk:(kb + 1) * tk, :].T.astype(BF16)
        tasks.append(_sb_task(state, 0, get_q, get_k, get_vt, lmat, None))
    _emit_pipelined(tasks, SB_SAMPLE_LAG)
    o_ref[0] = _pair_out(state["acc"][0]).astype(BF16)


def _sb_sample_call(q, kn, vtn, kp, vp, tk):
    b, t, w = q.shape
    assert 2 * t == LANES
    tp = kp.shape[1]
    new = pl.BlockSpec((1, t, LANES), lambda i, h: (i, 0, h))
    past = pl.BlockSpec((1, tp, LANES), lambda i, h: (i, 0, h))
    return pl.pallas_call(
        functools.partial(_sb_sample_kernel, tk=tk),
        grid=(b, w // LANES),
        in_specs=[new, new, pl.BlockSpec((1, 1, LANES, t), lambda i, h: (i, 0, h, 0)), past, past,
                  pl.BlockSpec((t, t), lambda i, h: (0, 0)),
                  pl.BlockSpec((tk, tk), lambda i, h: (0, 0))],
        out_specs=new,
        out_shape=jax.ShapeDtypeStruct((b, t, w), BF16),
        compiler_params=_cparams(2),
        name="sb_sample",
    )(q, kn, vtn, kp, vp, _suffix_matrix(t), _suffix_matrix(tk))


def _mla_task(state, c, get_q, get_k, get_vt, masked):
    t = {}

    def scores():
        t["s"] = _dot_nt(get_k(), get_q())

    def probs():
        s = t.pop("s")
        if masked is not None:
            s = jnp.where(masked, s, MASKED_SCORE)
        m = state["m"][c]
        m_new = jnp.maximum(m, jnp.max(s, axis=0, keepdims=True))
        p = jnp.exp2(s - m_new)
        alpha = jnp.exp2(m - m_new)
        state["m"][c] = m_new
        state["l"][c] = alpha * state["l"][c] + jnp.sum(p, axis=0, keepdims=True)
        t["alpha"] = alpha
        t["pv"] = _dot(get_vt(), p.astype(BF16))

    def accumulate():
        state["acc"][c] = t.pop("alpha") * state["acc"][c] + t.pop("pv")

    return [scores, probs, accumulate]


def _mla_state(n, nd, nq):
    return {"m": [jnp.full((1, nq), MASKED_SCORE, F32)] * n, "l": [jnp.zeros((1, nq), F32)] * n,
            "acc": [jnp.zeros((nd, nq), F32)] * n}


def _mla_prompt_kernel(q_ref, k_ref, vt_ref, o_ref, *, tb):
    nt = q_ref.shape[1] // tb
    key = lax.broadcasted_iota(jnp.int32, (tb, tb), 0)
    qry = lax.broadcasted_iota(jnp.int32, (tb, tb), 1)
    visible = (key // CHUNK) <= (qry // CHUNK)
    state = _mla_state(2 * nt, HEAD, tb)

    def unit(i, j, h):
        def write_tile():
            outs = [state["acc"][2 * i + hh] * (1.0 / state["l"][2 * i + hh]) for hh in range(2)]
            o_ref[0, i * tb:(i + 1) * tb, :] = jnp.concatenate(outs, axis=0).T.astype(BF16)

        q = lambda: q_ref[0, i * tb:(i + 1) * tb, h * LANES:(h + 1) * LANES]
        get_k = lambda: k_ref[0, j * tb:(j + 1) * tb, h * LANES:(h + 1) * LANES]
        get_vt = lambda: vt_ref[0, j, h * HEAD:(h + 1) * HEAD, :]
        stages = _mla_task(state, 2 * i + h, q, get_k, get_vt, visible if i == j else None)
        return stages + [write_tile] if (j == 0 and h == 1) else stages

    _emit_pipelined([unit(i, j, h) for i, j in _causal_units(nt) for h in range(2)], MLA_LAG)


def _mla_prompt_call(q, k, vt, tb):
    b, t, w2 = q.shape
    w = w2 // 2
    return pl.pallas_call(
        functools.partial(_mla_prompt_kernel, tb=tb),
        grid=(b, w // LANES),
        in_specs=[pl.BlockSpec((1, t, 2 * LANES), lambda i, h: (i, 0, h)),
                  pl.BlockSpec((1, t, 2 * LANES), lambda i, h: (i, 0, h)),
                  pl.BlockSpec((1, t // tb, LANES, tb), lambda i, h: (i, 0, h, 0))],
        out_specs=pl.BlockSpec((1, t, LANES), lambda i, h: (i, 0, h)),
        out_shape=jax.ShapeDtypeStruct((b, t, w), BF16),
        compiler_params=_cparams(2),
        name="mla_prompt",
    )(q, k, vt)


def _mla_sample_kernel(q_ref, kn_ref, vtn_ref, kp_ref, vtp_ref, o_ref, *, tk, new_mask):
    tq = q_ref.shape[1]
    n_past = kp_ref.shape[1] // tk
    zeros = jnp.zeros((tq, LANES), BF16)
    q0 = jnp.concatenate([q_ref[0, :, :LANES], zeros], axis=1)
    q1 = jnp.concatenate([zeros, q_ref[0, :, LANES:]], axis=1)
    qrows = _pair_rows(q0, q1)
    mask = None
    if new_mask is not None:
        qc, kc = new_mask
        key = lax.broadcasted_iota(jnp.int32, (tq, 2 * tq), 0) + kc
        qry = lax.broadcasted_iota(jnp.int32, (tq, 2 * tq), 1) % tq + qc
        mask = (key // CHUNK) <= (qry // CHUNK)
    state = _mla_state(1, LANES, 2 * tq)
    get_q = lambda: qrows
    tasks = [_mla_task(state, 0, get_q, lambda: kn_ref[0], lambda: vtn_ref[0, 0], mask)]
    for kb in range(n_past):
        get_k = lambda kb=kb: kp_ref[0, kb * tk:(kb + 1) * tk, :]
        get_vt = lambda kb=kb: vtp_ref[0, kb]
        tasks.append(_mla_task(state, 0, get_q, get_k, get_vt, None))
    _emit_pipelined(tasks, MLA_SAMPLE_LAG)
    o_ref[0] = _pair_out(state["acc"][0] * (1.0 / state["l"][0])).astype(BF16)


def _mla_sample_call(q, kn, vtn, kp, vtp, tk, past_len):
    b, t, w2 = q.shape
    w = w2 // 2
    assert 2 * t == LANES
    tp = kp.shape[1]
    assert past_len % CHUNK == 0
    new_mask = None if t <= CHUNK else (past_len, past_len)
    return pl.pallas_call(
        functools.partial(_mla_sample_kernel, tk=tk, new_mask=new_mask),
        grid=(b, w // LANES),
        in_specs=[pl.BlockSpec((1, t, 2 * LANES), lambda i, h: (i, 0, h)),
                  pl.BlockSpec((1, t, 2 * LANES), lambda i, h: (i, 0, h)),
                  pl.BlockSpec((1, 1, LANES, t), lambda i, h: (i, 0, h, 0)),
                  pl.BlockSpec((1, tp, 2 * LANES), lambda i, h: (i, 0, h)),
                  pl.BlockSpec((1, tp // tk, LANES, tk), lambda i, h: (i, 0, h, 0))],
        out_specs=pl.BlockSpec((1, t, LANES), lambda i, h: (i, 0, h)),
        out_shape=jax.ShapeDtypeStruct((b, t, w), BF16),
        compiler_params=_cparams(2),
        name="mla_sample",
    )(q, kn, vtn, kp, vtp)


def _past_kv_kernel(ckv_ref, kr_ref, wkn_ref, wv_ref, p_ref, kcat_ref, mvt_ref):
    cb = ckv_ref[0].astype(BF16)
    kr = kr_ref[0].astype(BF16)
    kcat_ref[0] = (_dot(cb, wkn_ref[...]) + _dot(kr, p_ref[...])).astype(BF16)
    mvt_ref[0, 0] = _dot(cb, wv_ref[...]).T.astype(BF16)


def _past_kv_call(ckv, kr, pw, tm):
    b, t, r = ckv.shape
    full = lambda a: pl.BlockSpec(a.shape, lambda i, j: (0,) * a.ndim)
    row = lambda n: pl.BlockSpec((1, tm, n), lambda i, j: (i, j, 0))
    place = pw["place"][:MLA_ROPE_DIM]
    return pl.pallas_call(
        _past_kv_kernel,
        grid=(b, t // tm),
        in_specs=[row(r), row(MLA_ROPE_DIM), full(pw["w_kn"]), full(pw["w_v"]), full(place)],
        out_specs=[row(1024), pl.BlockSpec((1, 1, 512, tm), lambda i, j: (i, j, 0, 0))],
        out_shape=[jax.ShapeDtypeStruct((b, t, 1024), BF16),
                   jax.ShapeDtypeStruct((b, t // tm, 512, tm), BF16)],
        compiler_params=_cparams(2),
        name="past_kv",
    )(ckv, kr, pw["w_kn"], pw["w_v"], place)


def _out_kernel(sbo_ref, mlao_ref, gs_ref, x_ref, mod_ref, w_ref, g_ref, y_ref):
    half = sbo_ref.shape[2]
    m0 = (sbo_ref[0].astype(F32) * gs_ref[0, :, :half].astype(F32)).astype(BF16)
    m1 = (mlao_ref[0].astype(F32) * gs_ref[0, :, half:].astype(F32)).astype(BF16)
    out = _dot(m0, w_ref[:half, :]) + _dot(m1, w_ref[half:, :])
    y_ref[0] = x_ref[0] + mod_ref[0, 2:3, :] * _rms(out, g_ref[...])


def _out_call(sbo, mlao, gs, x, mod, w_out16, post_g, tm):
    b, t, d = x.shape
    row = lambda n: pl.BlockSpec((1, tm, n), lambda i, j: (i, j, 0))
    full = lambda a: pl.BlockSpec(a.shape, lambda i, j: (0,) * a.ndim)
    return pl.pallas_call(
        _out_kernel,
        grid=(b, t // tm),
        in_specs=[row(512), row(512), row(1024), row(d),
                  pl.BlockSpec((1, 3, d), lambda i, j: (i, 0, 0)), full(w_out16), full(post_g)],
        out_specs=row(d),
        out_shape=jax.ShapeDtypeStruct((b, t, d), F32),
        compiler_params=_cparams(2),
        name="out_proj",
    )(sbo, mlao, gs, x, mod, w_out16, post_g)


def _prep_weights(pre_g, w_in, q_g, w_uq, kv_g, w_ukv):
    d = w_in.shape[0]
    n_heads = w_ukv.shape[1] // (MLA_NOPE_DIM + MLA_V_DIM)
    half = MLA_ROPE_DIM // 2
    sizes = (512, 512, 512, 512, w_uq.shape[0], w_ukv.shape[0], MLA_ROPE_DIM, 512)
    starts = [0]
    for s in sizes:
        starts.append(starts[-1] + s)
    cols = lambda i: w_in[:, starts[i]:starts[i + 1]]
    kr = cols(6)
    kr_swapped = jnp.concatenate([-kr[:, half:], kr[:, :half]], axis=1)
    zpad = jnp.zeros((d, LANES - MLA_ROPE_DIM), w_in.dtype)
    w_aug = jnp.concatenate([cols(0), cols(1), cols(2), cols(3), cols(4), cols(5), cols(7),
                             kr, zpad, kr_swapped, zpad], axis=1).astype(BF16)
    assert w_aug.shape[1] == _C_END

    r_q = w_uq.shape[0]
    uq = w_uq.reshape(r_q, n_heads, MLA_NOPE_DIM + MLA_ROPE_DIM)
    nope, rope = uq[..., :MLA_NOPE_DIM], uq[..., MLA_NOPE_DIM:]
    zq = jnp.zeros((r_q, n_heads, LANES - MLA_NOPE_DIM - MLA_ROPE_DIM), w_uq.dtype)
    w_a = jnp.concatenate([nope, rope, zq], axis=-1).reshape(r_q, n_heads * LANES)
    rope_swapped = jnp.concatenate([-rope[..., half:], rope[..., :half]], axis=-1)
    w_b = jnp.concatenate([jnp.zeros_like(nope), rope_swapped, zq], axis=-1).reshape(r_q, n_heads * LANES)
    w_ab = jnp.concatenate([w_a, w_b], axis=1).astype(BF16)

    r_kv = w_ukv.shape[0]
    ukv = w_ukv.reshape(r_kv, n_heads, MLA_NOPE_DIM + MLA_V_DIM)
    w_kn = jnp.concatenate([ukv[..., :MLA_NOPE_DIM],
                            jnp.zeros((r_kv, n_heads, LANES - MLA_NOPE_DIM), w_ukv.dtype)],
                           axis=-1).reshape(r_kv, n_heads * LANES).astype(BF16)
    w_v = ukv[..., MLA_NOPE_DIM:].reshape(r_kv, n_heads * MLA_V_DIM).astype(BF16)

    src = lax.broadcasted_iota(jnp.int32, (LANES, n_heads * LANES), 0)
    dst = lax.broadcasted_iota(jnp.int32, (LANES, n_heads * LANES), 1)
    place = ((dst % LANES == src + MLA_NOPE_DIM) & (src < MLA_ROPE_DIM)).astype(BF16)

    return dict(pre_g=pre_g[None, :], w_aug=w_aug, q_g=q_g[None, :], w_ab=w_ab,
                kv_g=kv_g[None, :], w_kn=w_kn, w_v=w_v, place=place)


def _rope_tables(pos):
    r = MLA_ROPE_DIM
    inv_freq = ROPE_THETA ** (-jnp.arange(0, r, 2, dtype=F32) / r)
    ang = pos.astype(F32)[:, None] * inv_freq[None, :]
    cos = jnp.concatenate([jnp.cos(ang), jnp.cos(ang)], axis=1)
    sin = jnp.concatenate([jnp.sin(ang), jnp.sin(ang)], axis=1)
    t = pos.shape[0]
    scale = (MLA_NOPE_DIM + MLA_ROPE_DIM) ** -0.5 * LOG2E
    zq = jnp.zeros((t, LANES - MLA_NOPE_DIM - r), F32)
    cq = jnp.concatenate([jnp.full((t, MLA_NOPE_DIM), scale, F32), cos * scale, zq], axis=1)
    sq = jnp.concatenate([jnp.zeros((t, MLA_NOPE_DIM), F32), sin * scale, zq], axis=1)
    zk = jnp.zeros((t, LANES - r), F32)
    ck = jnp.concatenate([cos, zk], axis=1)
    sk = jnp.concatenate([sin, zk], axis=1)
    return dict(cq=cq, sq=sq, ck=ck, sk=sk)


PROMPT_T = 256
OUT_TM = 512
PAST_TK = 256
SB_LAG = 3
MLA_LAG = 4
SB_SAMPLE_LAG = 4
MLA_SAMPLE_LAG = 6


def kernel(x_prompt, x_sample, cache_sb_k, cache_sb_v, cache_mla_ckv, cache_mla_krope, c_prompt, c_sample, ada_w, ada_b, pre_norm_g, w_in, q_norm_g, w_uq, kv_norm_g, w_ukv, w_out, post_norm_g):
    depth = ada_w.shape[0]
    bp, tp, d = x_prompt.shape
    bs, ts, _ = x_sample.shape
    past_len = cache_sb_k.shape[2]
    n_sb = cache_sb_k.shape[3]

    tabs_p = _rope_tables(jnp.arange(tp, dtype=jnp.int32))
    tabs_s = _rope_tables(past_len + jnp.arange(ts, dtype=jnp.int32))
    c_all = jnp.concatenate([c_prompt, c_sample], axis=0)

    yp, ys = x_prompt, x_sample
    new_p, new_s = [], []
    for l in range(depth):
        pw = _prep_weights(pre_norm_g[l], w_in[l], q_norm_g[l], w_uq[l], kv_norm_g[l], w_ukv[l])
        w_out16 = w_out[l].astype(BF16)
        post_g = post_norm_g[l][None, :]
        mod = _mod_call(c_all, ada_w[l].astype(BF16), ada_b[l][None, :])
        mod_p = mod[:bp].reshape(bp, 3, d)
        mod_s = mod[bp:].reshape(bs, 3, d)

        (sbq, sbk32, sbk16, sbv32, sbvt, gs, qcat, ckv, kcat, mvt, kr) = _proj_call(
            yp, mod_p, pw, tabs_p, PROMPT_T)
        sbo = _sb_prompt_call(sbq, sbk16, sbvt, PROMPT_T)
        mlao = _mla_prompt_call(qcat, kcat, mvt, PROMPT_T)
        yp = _out_call(sbo, mlao, gs, yp, mod_p, w_out16, post_g, OUT_TM)
        new_p.append((sbk32.reshape(bp, tp, n_sb, SB_HEAD_DIM), sbv32.reshape(bp, tp, n_sb, SB_HEAD_DIM),
                      ckv, kr))

        (sbq, sbk32, sbk16, sbv32, sbvt, gs, qcat, ckv, kcat, mvt, kr) = _proj_call(
            ys, mod_s, pw, tabs_s, ts)
        kp = cache_sb_k[l].reshape(bs, past_len, n_sb * SB_HEAD_DIM)
        vp = cache_sb_v[l].reshape(bs, past_len, n_sb * SB_HEAD_DIM)
        sbo = _sb_sample_call(sbq, sbk16, sbvt, kp, vp, PAST_TK)
        kcat_p, mvt_p = _past_kv_call(cache_mla_ckv[l], cache_mla_krope[l], pw, PAST_TK)
        mlao = _mla_sample_call(qcat, kcat, mvt, kcat_p, mvt_p, PAST_TK, past_len)
        ys = _out_call(sbo, mlao, gs, ys, mod_s, w_out16, post_g, ts)
        new_s.append((sbk32.reshape(bs, ts, n_sb, SB_HEAD_DIM), sbv32.reshape(bs, ts, n_sb, SB_HEAD_DIM),
                      ckv, kr))

    stack = lambda items, i: items[0][i][None] if depth == 1 else jnp.stack([it[i] for it in items])
    return (yp, ys,
            stack(new_p, 0), stack(new_p, 1), stack(new_p, 2), stack(new_p, 3),
            stack(new_s, 0), stack(new_s, 1), stack(new_s, 2), stack(new_s, 3))
```

```python
import functools

import jax
import jax.numpy as jnp
from jax import lax
from jax.experimental import pallas as pl
from jax.experimental.pallas import tpu as pltpu

F32 = jnp.float32
BF16 = jnp.bfloat16

CHUNK = 64
SB_HEAD_DIM = 64
MLA_V_DIM = 64
MLA_NOPE_DIM = 64
MLA_ROPE_DIM = 32
ROPE_THETA = 10000.0
EPS = 1e-6
LOG2E = 1.4426950408889634
LANES = 128
HEAD = 64
MASKED_SCORE = -1e30

VMEM_LIMIT = 56 * 1024 * 1024


def _cparams(n_grid, flags=None):
    return pltpu.CompilerParams(dimension_semantics=("arbitrary",) * n_grid,
                                vmem_limit_bytes=VMEM_LIMIT, flags=flags)


def _silu(x):
    return x * (1.0 / (1.0 + jnp.exp(-x)))


def _rms(x, g):
    return x * lax.rsqrt(jnp.mean(x * x, axis=-1, keepdims=True) + EPS) * g


def _dot(a, b):
    return jnp.dot(a, b, preferred_element_type=F32)


def _dot_nt(a, b):
    return lax.dot_general(a, b, (((1,), (1,)), ((), ())), preferred_element_type=F32)


def _mod_kernel(c_ref, w_ref, b_ref, o_ref):
    s = _silu(c_ref[...]).astype(BF16)
    o_ref[...] = _dot(s, w_ref[...]) + b_ref[...]


def _mod_call(c_all, ada_w16, ada_b):
    n, d3 = c_all.shape[0], ada_w16.shape[1]
    return pl.pallas_call(
        _mod_kernel,
        out_shape=jax.ShapeDtypeStruct((n, d3), F32),
        compiler_params=pltpu.CompilerParams(vmem_limit_bytes=VMEM_LIMIT),
        name="ada_mod",
    )(c_all, ada_w16, ada_b)


_C_SBQ, _C_SBK, _C_SBV, _C_SBG = 0, 512, 1024, 1536
_C_CQ, _C_CKV, _C_MLAG, _C_KR, _C_END = 2048, 2432, 2688, 3200, 3328
_HALF_ROPE = MLA_ROPE_DIM // 2


def _proj_kernel(x_ref, mod_ref, pg_ref, w_ref, qg_ref, wa_ref, kvg_ref, wkn_ref, wv_ref,
                 cq_ref, s1_ref, s2_ref, ck_ref, sk_ref,
                 sbq_ref, sbk32_ref, sbk16_ref, sbv32_ref, sbvt_ref, gs_ref, qcat_ref,
                 ckv_ref, kcat_ref, mvt_ref, kr_ref, krp_ref):
    x = x_ref[0]
    rs = lax.rsqrt(jnp.mean(x * x, axis=-1, keepdims=True) + EPS)
    g = pg_ref[...] * (1.0 + mod_ref[0, 1:2, :])
    hb = ((x * rs) * g + mod_ref[0, 0:1, :]).astype(BF16)

    def proj(a, b):
        return _dot(hb, w_ref[:, a:b])

    sbq_ref[0] = (proj(_C_SBQ, _C_SBK) * (SB_HEAD_DIM ** -0.5 * LOG2E)).astype(BF16)
    k = proj(_C_SBK, _C_SBV)
    sbk32_ref[0] = k
    sbk16_ref[0] = k.astype(BF16)
    v = proj(_C_SBV, _C_SBG)
    sbv32_ref[0] = v
    sbvt_ref[0, 0] = v.T.astype(BF16)
    gs_ref[0, :, 0:512] = _silu(proj(_C_SBG, _C_CQ)).astype(BF16)
    gs_ref[0, :, 512:1024] = _silu(proj(_C_MLAG, _C_KR)).astype(BF16)

    cqn = _rms(proj(_C_CQ, _C_CKV), qg_ref[...]).astype(BF16)
    qa = _dot(cqn, wa_ref[...])
    cq_t, s1_t, s2_t = cq_ref[...], s1_ref[...], s2_ref[...]
    n_heads = qcat_ref.shape[2] // LANES
    for h in range(n_heads):
        a = qa[:, h * LANES:(h + 1) * LANES]
        up = pltpu.roll(a, _HALF_ROPE, 1)
        down = pltpu.roll(a, LANES - _HALF_ROPE, 1)
        qcat_ref[0, :, h * LANES:(h + 1) * LANES] = (a * cq_t + up * s1_t + down * s2_t).astype(BF16)

    ckvn = _rms(proj(_C_CKV, _C_MLAG), kvg_ref[...])
    ckv_ref[0] = ckvn
    cb = ckvn.astype(BF16)
    r = proj(_C_KR, _C_END)
    kro = r * ck_ref[...] + pltpu.roll(r, LANES - MLA_ROPE_DIM, 1) * sk_ref[...]
    kr_ref[0] = kro[:, :MLA_ROPE_DIM]
    krp_ref[0] = kro.astype(BF16)
    slot = pltpu.roll(kro, MLA_NOPE_DIM, 1)
    kn = _dot(cb, wkn_ref[...])
    for h in range(n_heads):
        kcat_ref[0, :, h * LANES:(h + 1) * LANES] = (kn[:, h * LANES:(h + 1) * LANES] + slot).astype(BF16)
    mvt_ref[0, 0] = _dot(cb, wv_ref[...]).T.astype(BF16)


def _proj_call(x, mod, pw, tabs, tm):
    b, t, d = x.shape
    nt = t // tm
    row = lambda n: pl.BlockSpec((1, tm, n), lambda i, j: (i, j, 0))
    colt = pl.BlockSpec((1, 1, 512, tm), lambda i, j: (i, j, 0, 0))
    full = lambda a: pl.BlockSpec(a.shape, lambda i, j: (0,) * a.ndim)
    tab = pl.BlockSpec((tm, LANES), lambda i, j: (j, 0))
    o = lambda n, dt: jax.ShapeDtypeStruct((b, t, n), dt)
    ot = jax.ShapeDtypeStruct((b, nt, 512, tm), BF16)
    return pl.pallas_call(
        _proj_kernel,
        grid=(b, nt),
        in_specs=[row(d), pl.BlockSpec((1, 3, d), lambda i, j: (i, 0, 0)), full(pw["pre_g"]),
                  full(pw["w_aug"]), full(pw["q_g"]), full(pw["w_a"]), full(pw["kv_g"]),
                  full(pw["w_kn"]), full(pw["w_v"]), tab, tab, tab, tab, tab],
        out_specs=[row(512), row(512), row(512), row(512), colt, row(1024), row(1024),
                   row(256), row(1024), colt, row(MLA_ROPE_DIM), row(LANES)],
        out_shape=[o(512, BF16), o(512, F32), o(512, BF16), o(512, F32), ot,
                   o(1024, BF16), o(1024, BF16), o(256, F32), o(1024, BF16), ot,
                   o(MLA_ROPE_DIM, F32), o(LANES, BF16)],
        compiler_params=_cparams(2),
        name="in_proj",
    )(x, mod, pw["pre_g"], pw["w_aug"], pw["q_g"], pw["w_a"], pw["kv_g"], pw["w_kn"], pw["w_v"],
      tabs["cq"], tabs["s1"], tabs["s2"], tabs["ck"], tabs["sk"])


def _emit_pipelined(tasks, lag):
    n_slots = max(j + (len(stages) - 1) * lag for j, stages in enumerate(tasks)) + 1
    for slot in range(n_slots):
        for j, stages in enumerate(tasks):
            s, rem = divmod(slot - j, lag)
            if slot >= j and rem == 0 and s < len(stages):
                stages[s]()


def _sb_task(state, c, get_q, get_k, get_vt, lmat, masked):
    t = {}

    def scores():
        t["z"] = _dot_nt(get_k(), get_q())

    def logs():
        z = t.pop("z")
        if masked is not None:
            z = jnp.where(masked, z, MASKED_SCORE)
        low = jnp.minimum(z, 0.0)
        neg_relu = low - z
        l = jnp.log2(1.0 + jnp.exp2(low + neg_relu))
        t["lb"] = low - l
        lr = neg_relu - l
        t["lr0"] = lr[0:1, :]
        t["tail"] = _dot(lmat, lr.astype(BF16))

    def weights():
        tail = t.pop("tail")
        w = jnp.exp2(t.pop("lb") + tail)
        r = state["r"][c]
        t["scale"] = jnp.exp2(r)
        state["r"][c] = r + (tail[0:1, :] + t.pop("lr0"))
        t["pv"] = _dot(get_vt(), w.astype(BF16))

    def accumulate():
        state["acc"][c] = state["acc"][c] + t.pop("pv") * t.pop("scale")

    return [scores, logs, weights, accumulate]


def _split_heads(qp):
    lane = lax.broadcasted_iota(jnp.int32, qp.shape, 1)
    qf = qp.astype(F32)
    first = lane < HEAD
    return jnp.where(first, qf, 0.0).astype(BF16), jnp.where(first, 0.0, qf).astype(BF16)


def _causal_units(nt):
    return [(i, i - dist) for dist in range(nt) for i in range(dist, nt)]


def _sb_prompt_kernel(q_ref, k_ref, vt_ref, l_ref, o_ref, *, tb):
    nt = q_ref.shape[1] // tb
    key = lax.broadcasted_iota(jnp.int32, (tb, tb), 0)
    qry = lax.broadcasted_iota(jnp.int32, (tb, tb), 1)
    causal = key < qry
    lmat = l_ref[...]
    qs = {}

    def unit(i, j, h):
        def get_q():
            if (i, h) not in qs:
                qs[(i, 0)], qs[(i, 1)] = _split_heads(q_ref[0, i * tb:(i + 1) * tb, :])
            return qs[(i, h)]

        def write_tile():
            pair = jnp.concatenate([state["acc"][2 * i], state["acc"][2 * i + 1]], axis=0)
            o_ref[0, i * tb:(i + 1) * tb, :] = pair.T.astype(BF16)

        get_k = lambda: k_ref[0, j * tb:(j + 1) * tb, :]
        get_vt = lambda: vt_ref[0, j, h * HEAD:(h + 1) * HEAD, :]
        stages = _sb_task(state, 2 * i + h, get_q, get_k, get_vt, lmat, causal if i == j else None)
        return stages + [write_tile] if (j == 0 and h == 1) else stages

    state = {"r": [jnp.zeros((1, tb), F32)] * (2 * nt), "acc": [jnp.zeros((HEAD, tb), F32)] * (2 * nt)}
    _emit_pipelined([unit(i, j, h) for i, j in _causal_units(nt) for h in range(2)], SB_LAG)


def _suffix_matrix(n):
    s = lax.broadcasted_iota(jnp.int32, (n, n), 0)
    j = lax.broadcasted_iota(jnp.int32, (n, n), 1)
    return (j > s).astype(BF16)


def _sb_prompt_call(q, k, vt, tb):
    b, t, w = q.shape
    return pl.pallas_call(
        functools.partial(_sb_prompt_kernel, tb=tb),
        grid=(b, w // LANES),
        in_specs=[pl.BlockSpec((1, t, LANES), lambda i, h: (i, 0, h)),
                  pl.BlockSpec((1, t, LANES), lambda i, h: (i, 0, h)),
                  pl.BlockSpec((1, t // tb, LANES, tb), lambda i, h: (i, 0, h, 0)),
                  pl.BlockSpec((tb, tb), lambda i, h: (0, 0))],
        out_specs=pl.BlockSpec((1, t, LANES), lambda i, h: (i, 0, h)),
        out_shape=jax.ShapeDtypeStruct((b, t, w), BF16),
        compiler_params=_cparams(2),
        name="sb_prompt",
    )(q, k, vt, _suffix_matrix(tb))


def _pair_rows(q_lo, q_hi):
    return jnp.concatenate([q_lo, q_hi], axis=0)


def _pair_out(acc):
    acc_t = acc.T
    lane = lax.broadcasted_iota(jnp.int32, (HEAD, LANES), 1)
    return jnp.where(lane < HEAD, acc_t[:HEAD], acc_t[HEAD:])


def _sb_sample_kernel(q_ref, kn_ref, vtn_ref, ktp_ref, vtp_ref, ls_ref, l_ref, o_ref, *, tk):
    tq = q_ref.shape[1]
    n_past = ktp_ref.shape[2] // tk
    qrows = _pair_rows(*_split_heads(q_ref[0]))
    key = lax.broadcasted_iota(jnp.int32, (tq, 2 * tq), 0)
    qry = lax.broadcasted_iota(jnp.int32, (tq, 2 * tq), 1) % tq
    causal = key < qry
    lmat = l_ref[...]
    state = {"r": [jnp.zeros((1, 2 * tq), F32)], "acc": [jnp.zeros((LANES, 2 * tq), F32)]}
    get_q = lambda: qrows
    tasks = [_sb_task(state, 0, get_q, lambda: kn_ref[0], lambda: vtn_ref[0, 0], ls_ref[...], causal)]
    for kb in reversed(range(n_past)):
        get_k = lambda kb=kb: ktp_ref[0, :, kb * tk:(kb + 1) * tk].T.astype(BF16)
        get_vt = lambda kb=kb: vtp_ref[0, :, kb * tk:(kb + 1) * tk].astype(BF16)
        tasks.append(_sb_task(state, 0, get_q, get_k, get_vt, lmat, None))
    _emit_pipelined(tasks, SB_SAMPLE_LAG)
    o_ref[0] = _pair_out(state["acc"][0]).astype(BF16)


def _sb_sample_call(q, kn, vtn, ktp, vtp, tk):
    b, t, w = q.shape
    assert 2 * t == LANES
    tp = ktp.shape[2]
    new = pl.BlockSpec((1, t, LANES), lambda i, h: (i, 0, h))
    past = pl.BlockSpec((1, LANES, tp), lambda i, h: (i, h, 0))
    return pl.pallas_call(
        functools.partial(_sb_sample_kernel, tk=tk),
        grid=(b, w // LANES),
        in_specs=[new, new, pl.BlockSpec((1, 1, LANES, t), lambda i, h: (i, 0, h, 0)), past, past,
                  pl.BlockSpec((t, t), lambda i, h: (0, 0)),
                  pl.BlockSpec((tk, tk), lambda i, h: (0, 0))],
        out_specs=new,
        out_shape=jax.ShapeDtypeStruct((b, t, w), BF16),
        compiler_params=_cparams(2),
        name="sb_sample",
    )(q, kn, vtn, ktp, vtp, _suffix_matrix(t), _suffix_matrix(tk))


def _mla_task(state, c, get_q, get_k, get_vt, masked):
    t = {}

    def scores():
        t["s"] = _dot_nt(get_k(), get_q())

    def probs():
        s = t.pop("s")
        if masked is not None:
            s = jnp.where(masked, s, MASKED_SCORE)
        m = state["m"][c]
        m_new = jnp.maximum(m, jnp.max(s, axis=0, keepdims=True))
        p = jnp.exp2(s - m_new)
        alpha = jnp.exp2(m - m_new)
        state["m"][c] = m_new
        state["l"][c] = alpha * state["l"][c] + jnp.sum(p, axis=0, keepdims=True)
        t["alpha"] = alpha
        t["pv"] = _dot(get_vt(), p.astype(BF16))

    def accumulate():
        state["acc"][c] = t.pop("alpha") * state["acc"][c] + t.pop("pv")

    return [scores, probs, accumulate]


def _mla_state(n, nd, nq):
    return {"m": [jnp.full((1, nq), MASKED_SCORE, F32)] * n, "l": [jnp.zeros((1, nq), F32)] * n,
            "acc": [jnp.zeros((nd, nq), F32)] * n}


def _mla_prompt_kernel(q_ref, k_ref, vt_ref, o_ref, *, tb):
    nt = q_ref.shape[1] // tb
    key = lax.broadcasted_iota(jnp.int32, (tb, tb), 0)
    qry = lax.broadcasted_iota(jnp.int32, (tb, tb), 1)
    visible = (key // CHUNK) <= (qry // CHUNK)
    state = _mla_state(2 * nt, HEAD, tb)

    def unit(i, j, h):
        def write_tile():
            outs = [state["acc"][2 * i + hh] * (1.0 / state["l"][2 * i + hh]) for hh in range(2)]
            o_ref[0, i * tb:(i + 1) * tb, :] = jnp.concatenate(outs, axis=0).T.astype(BF16)

        q = lambda: q_ref[0, i * tb:(i + 1) * tb, h * LANES:(h + 1) * LANES]
        get_k = lambda: k_ref[0, j * tb:(j + 1) * tb, h * LANES:(h + 1) * LANES]
        get_vt = lambda: vt_ref[0, j, h * HEAD:(h + 1) * HEAD, :]
        stages = _mla_task(state, 2 * i + h, q, get_k, get_vt, visible if i == j else None)
        return stages + [write_tile] if (j == 0 and h == 1) else stages

    _emit_pipelined([unit(i, j, h) for i, j in _causal_units(nt) for h in range(2)], MLA_LAG)


def _mla_prompt_call(q, k, vt, tb):
    b, t, w2 = q.shape
    w = w2 // 2
    return pl.pallas_call(
        functools.partial(_mla_prompt_kernel, tb=tb),
        grid=(b, w // LANES),
        in_specs=[pl.BlockSpec((1, t, 2 * LANES), lambda i, h: (i, 0, h)),
                  pl.BlockSpec((1, t, 2 * LANES), lambda i, h: (i, 0, h)),
                  pl.BlockSpec((1, t // tb, LANES, tb), lambda i, h: (i, 0, h, 0))],
        out_specs=pl.BlockSpec((1, t, LANES), lambda i, h: (i, 0, h)),
        out_shape=jax.ShapeDtypeStruct((b, t, w), BF16),
        compiler_params=_cparams(2),
        name="mla_prompt",
    )(q, k, vt)


def _mla_sample_kernel(q_ref, cn_ref, krn_ref, cp_ref, krtp_ref, wknt_ref, sel_ref, wvf_ref, o_ref,
                       *, tk):
    tq = q_ref.shape[1]
    n_heads = wknt_ref.shape[0]
    n_past = cp_ref.shape[1] // tk
    slots = [q_ref[0, :, h * LANES:(h + 1) * LANES] for h in range(n_heads)]
    qa = jnp.concatenate([_dot(s, wknt_ref[h]) for h, s in enumerate(slots)], axis=0).astype(BF16)
    qr = jnp.concatenate([_dot(s, sel_ref[...]) for s in slots], axis=0).astype(BF16)
    nq = n_heads * tq
    state = {"m": jnp.full((nq, 1), MASKED_SCORE, F32), "l": jnp.zeros((nq, 1), F32),
             "acc": jnp.zeros((nq, qa.shape[1]), F32)}

    def unit(get_latent, rope_scores):
        t = {}

        def scores():
            t["lat"] = get_latent()
            t["s"] = _dot_nt(qa, t["lat"]) + rope_scores()

        def probs():
            s = t.pop("s")
            m = state["m"]
            m_new = jnp.maximum(m, jnp.max(s, axis=1, keepdims=True))
            p = jnp.exp2(s - m_new)
            alpha = jnp.exp2(m - m_new)
            state["m"] = m_new
            state["l"] = alpha * state["l"] + jnp.sum(p, axis=1, keepdims=True)
            t["alpha"] = alpha
            t["pv"] = _dot(p.astype(BF16), t.pop("lat"))

        def accumulate():
            state["acc"] = t.pop("alpha") * state["acc"] + t.pop("pv")

        return [scores, probs, accumulate]

    tasks = [unit(lambda: cn_ref[0].astype(BF16), lambda: _dot_nt(qr, krn_ref[0]))]
    pad = jnp.zeros((LANES - MLA_ROPE_DIM, tk), BF16)
    for kb in range(n_past):
        get_latent = lambda kb=kb: cp_ref[0, kb * tk:(kb + 1) * tk, :].astype(BF16)
        rope_scores = lambda kb=kb: _dot(qr, jnp.concatenate(
            [krtp_ref[0, :, kb * tk:(kb + 1) * tk].astype(BF16), pad], axis=0))
        tasks.append(unit(get_latent, rope_scores))
    _emit_pipelined(tasks, MLA_SAMPLE_LAG)
    lat_out = (state["acc"] * (1.0 / state["l"])).astype(BF16)
    out = _dot(lat_out[0:tq], wvf_ref[0])
    for h in range(1, n_heads):
        out = out + _dot(lat_out[h * tq:(h + 1) * tq], wvf_ref[h])
    o_ref[0] = out.astype(BF16)


def _mla_sample_call(q, ckv_new, kr_new, ckv_past, krt_past, pw, tk, past_len):
    b, t, w = q.shape
    tp, r_kv = ckv_past.shape[1], ckv_past.shape[2]
    assert past_len % CHUNK == 0 and t <= CHUNK
    full = lambda a: pl.BlockSpec(a.shape, lambda i: (0,) * a.ndim)
    return pl.pallas_call(
        functools.partial(_mla_sample_kernel, tk=tk),
        grid=(b,),
        in_specs=[pl.BlockSpec((1, t, w), lambda i: (i, 0, 0)),
                  pl.BlockSpec((1, t, r_kv), lambda i: (i, 0, 0)),
                  pl.BlockSpec((1, t, LANES), lambda i: (i, 0, 0)),
                  pl.BlockSpec((1, tp, r_kv), lambda i: (i, 0, 0)),
                  pl.BlockSpec((1, MLA_ROPE_DIM, tp), lambda i: (i, 0, 0)),
                  full(pw["w_knt"]), full(pw["rope_sel"]), full(pw["w_vf"])],
        out_specs=pl.BlockSpec((1, t, pw["w_vf"].shape[2]), lambda i: (i, 0, 0)),
        out_shape=jax.ShapeDtypeStruct((b, t, pw["w_vf"].shape[2]), BF16),
        compiler_params=_cparams(1),
        name="mla_sample",
    )(q, ckv_new, kr_new, ckv_past, krt_past, pw["w_knt"], pw["rope_sel"], pw["w_vf"])


def _out_kernel(sbo_ref, mlao_ref, gs_ref, x_ref, mod_ref, w_ref, g_ref, y_ref):
    half = sbo_ref.shape[2]
    m0 = (sbo_ref[0].astype(F32) * gs_ref[0, :, :half].astype(F32)).astype(BF16)
    m1 = (mlao_ref[0].astype(F32) * gs_ref[0, :, half:].astype(F32)).astype(BF16)
    out = _dot(m0, w_ref[:half, :]) + _dot(m1, w_ref[half:, :])
    y_ref[0] = x_ref[0] + mod_ref[0, 2:3, :] * _rms(out, g_ref[...])


def _out_call(sbo, mlao, gs, x, mod, w_out16, post_g, tm):
    b, t, d = x.shape
    row = lambda n: pl.BlockSpec((1, tm, n), lambda i, j: (i, j, 0))
    full = lambda a: pl.BlockSpec(a.shape, lambda i, j: (0,) * a.ndim)
    return pl.pallas_call(
        _out_kernel,
        grid=(b, t // tm),
        in_specs=[row(512), row(512), row(1024), row(d),
                  pl.BlockSpec((1, 3, d), lambda i, j: (i, 0, 0)), full(w_out16), full(post_g)],
        out_specs=row(d),
        out_shape=jax.ShapeDtypeStruct((b, t, d), F32),
        compiler_params=_cparams(2),
        name="out_proj",
    )(sbo, mlao, gs, x, mod, w_out16, post_g)


def _prep_weights(pre_g, w_in, q_g, w_uq, kv_g, w_ukv):
    d = w_in.shape[0]
    n_heads = w_ukv.shape[1] // (MLA_NOPE_DIM + MLA_V_DIM)
    half = MLA_ROPE_DIM // 2
    sizes = (512, 512, 512, 512, w_uq.shape[0], w_ukv.shape[0], MLA_ROPE_DIM, 512)
    starts = [0]
    for s in sizes:
        starts.append(starts[-1] + s)
    cols = lambda i: w_in[:, starts[i]:starts[i + 1]]
    kr = cols(6)
    kr_swapped = jnp.concatenate([-kr[:, half:], kr[:, :half]], axis=1)
    zpad = jnp.zeros((d, LANES - 2 * MLA_ROPE_DIM), w_in.dtype)
    w_aug = jnp.concatenate([cols(0), cols(1), cols(2), cols(3), cols(4), cols(5), cols(7),
                             kr, kr_swapped, zpad], axis=1).astype(BF16)
    assert w_aug.shape[1] == _C_END

    r_q = w_uq.shape[0]
    uq = w_uq.reshape(r_q, n_heads, MLA_NOPE_DIM + MLA_ROPE_DIM)
    zq = jnp.zeros((r_q, n_heads, LANES - MLA_NOPE_DIM - MLA_ROPE_DIM), w_uq.dtype)
    w_a = jnp.concatenate([uq, zq], axis=-1).reshape(r_q, n_heads * LANES).astype(BF16)

    r_kv = w_ukv.shape[0]
    ukv = w_ukv.reshape(r_kv, n_heads, MLA_NOPE_DIM + MLA_V_DIM)
    uk, uv = ukv[..., :MLA_NOPE_DIM], ukv[..., MLA_NOPE_DIM:]
    zk = jnp.zeros((r_kv, n_heads, LANES - MLA_NOPE_DIM), w_ukv.dtype)
    w_kn = jnp.concatenate([uk, zk], axis=-1).reshape(r_kv, n_heads * LANES).astype(BF16)
    w_v = uv.reshape(r_kv, n_heads * MLA_V_DIM).astype(BF16)

    w_knt = jnp.transpose(jnp.concatenate([uk, zk], axis=-1), (1, 2, 0)).astype(BF16)
    eye = jnp.eye(n_heads, dtype=w_ukv.dtype)
    w_vf = jnp.einsum("rhd,hg->hrgd", uv, eye).reshape(n_heads, r_kv, n_heads * MLA_V_DIM).astype(BF16)
    src = lax.broadcasted_iota(jnp.int32, (LANES, LANES), 0)
    dst = lax.broadcasted_iota(jnp.int32, (LANES, LANES), 1)
    rope_sel = ((src == dst + MLA_NOPE_DIM) & (dst < MLA_ROPE_DIM)).astype(BF16)

    return dict(pre_g=pre_g[None, :], w_aug=w_aug, q_g=q_g[None, :], w_a=w_a,
                kv_g=kv_g[None, :], w_kn=w_kn, w_v=w_v, w_knt=w_knt, w_vf=w_vf, rope_sel=rope_sel)


def _rope_tables(pos):
    r = MLA_ROPE_DIM
    half = r // 2
    inv_freq = ROPE_THETA ** (-jnp.arange(0, r, 2, dtype=F32) / r)
    ang = pos.astype(F32)[:, None] * inv_freq[None, :]
    cos, sin = jnp.cos(ang), jnp.sin(ang)
    t = pos.shape[0]
    scale = (MLA_NOPE_DIM + MLA_ROPE_DIM) ** -0.5 * LOG2E
    z = lambda n: jnp.zeros((t, n), F32)
    cq = jnp.concatenate([jnp.full((t, MLA_NOPE_DIM), scale, F32), cos * scale, cos * scale,
                          z(LANES - MLA_NOPE_DIM - r)], axis=1)
    s1 = jnp.concatenate([z(MLA_NOPE_DIM + half), sin * scale, z(LANES - MLA_NOPE_DIM - r)], axis=1)
    s2 = jnp.concatenate([z(MLA_NOPE_DIM), -sin * scale, z(LANES - MLA_NOPE_DIM - half)], axis=1)
    ck = jnp.concatenate([cos, cos, z(LANES - r)], axis=1)
    sk = jnp.concatenate([sin, sin, z(LANES - r)], axis=1)
    return dict(cq=cq, s1=s1, s2=s2, ck=ck, sk=sk)


PROMPT_T = 256
OUT_TM = 512
PAST_TK = 256
SB_LAG = 2
MLA_LAG = 6
SB_SAMPLE_LAG = 4
MLA_SAMPLE_LAG = 6


def kernel(x_prompt, x_sample, cache_sb_k, cache_sb_v, cache_mla_ckv, cache_mla_krope, c_prompt, c_sample, ada_w, ada_b, pre_norm_g, w_in, q_norm_g, w_uq, kv_norm_g, w_ukv, w_out, post_norm_g):
    depth = ada_w.shape[0]
    bp, tp, d = x_prompt.shape
    bs, ts, _ = x_sample.shape
    past_len = cache_sb_k.shape[2]
    n_sb = cache_sb_k.shape[3]

    tabs_p = _rope_tables(jnp.arange(tp, dtype=jnp.int32))
    tabs_s = _rope_tables(past_len + jnp.arange(ts, dtype=jnp.int32))
    c_all = jnp.concatenate([c_prompt, c_sample], axis=0)

    yp, ys = x_prompt, x_sample
    new_p, new_s = [], []
    for l in range(depth):
        pw = _prep_weights(pre_norm_g[l], w_in[l], q_norm_g[l], w_uq[l], kv_norm_g[l], w_ukv[l])
        w_out16 = w_out[l].astype(BF16)
        post_g = post_norm_g[l][None, :]
        mod = _mod_call(c_all, ada_w[l].astype(BF16), ada_b[l][None, :])
        mod_p = mod[:bp].reshape(bp, 3, d)
        mod_s = mod[bp:].reshape(bs, 3, d)

        (sbq, sbk32, sbk16, sbv32, sbvt, gs, qcat, ckv, kcat, mvt, kr, _) = _proj_call(
            yp, mod_p, pw, tabs_p, PROMPT_T)
        sbo = _sb_prompt_call(sbq, sbk16, sbvt, PROMPT_T)
        mlao = _mla_prompt_call(qcat, kcat, mvt, PROMPT_T)
        yp = _out_call(sbo, mlao, gs, yp, mod_p, w_out16, post_g, OUT_TM)
        new_p.append((sbk32.reshape(bp, tp, n_sb, SB_HEAD_DIM), sbv32.reshape(bp, tp, n_sb, SB_HEAD_DIM),
                      ckv, kr))

        (sbq, sbk32, sbk16, sbv32, sbvt, gs, qcat, ckv, _, _, kr, krp) = _proj_call(
            ys, mod_s, pw, tabs_s, ts)
        ktp = jnp.transpose(cache_sb_k[l], (0, 2, 3, 1)).reshape(bs, n_sb * SB_HEAD_DIM, past_len)
        vtp = jnp.transpose(cache_sb_v[l], (0, 2, 3, 1)).reshape(bs, n_sb * SB_HEAD_DIM, past_len)
        sbo = _sb_sample_call(sbq, sbk16, sbvt, ktp, vtp, PAST_TK)
        krt_p = jnp.transpose(cache_mla_krope[l], (0, 2, 1))
        mlao = _mla_sample_call(qcat, ckv, krp, cache_mla_ckv[l], krt_p, pw, PAST_TK, past_len)
        ys = _out_call(sbo, mlao, gs, ys, mod_s, w_out16, post_g, ts)
        new_s.append((sbk32.reshape(bs, ts, n_sb, SB_HEAD_DIM), sbv32.reshape(bs, ts, n_sb, SB_HEAD_DIM),
                      ckv, kr))

    stack = lambda items, i: items[0][i][None] if depth == 1 else jnp.stack([it[i] for it in items])
    return (yp, ys,
            stack(new_p, 0), stack(new_p, 1), stack(new_p, 2), stack(new_p, 3),
            stack(new_s, 0), stack(new_s, 1), stack(new_s, 2), stack(new_s, 3))
```

```python
import functools

import jax
import jax.numpy as jnp
from jax import lax
from jax.experimental import pallas as pl
from jax.experimental.pallas import tpu as pltpu

F32 = jnp.float32
BF16 = jnp.bfloat16

CHUNK = 64
SB_HEAD_DIM = 64
MLA_V_DIM = 64
MLA_NOPE_DIM = 64
MLA_ROPE_DIM = 32
ROPE_THETA = 10000.0
EPS = 1e-6
LOG2E = 1.4426950408889634
LANES = 128
HEAD = 64
MASKED_SCORE = -1e30

VMEM_LIMIT = 56 * 1024 * 1024


def _cparams(n_grid, flags=None):
    return pltpu.CompilerParams(dimension_semantics=("arbitrary",) * n_grid,
                                vmem_limit_bytes=VMEM_LIMIT, flags=flags)


def _silu(x):
    return x * (1.0 / (1.0 + jnp.exp(-x)))


def _rms(x, g):
    return x * lax.rsqrt(jnp.mean(x * x, axis=-1, keepdims=True) + EPS) * g


def _dot(a, b):
    return jnp.dot(a, b, preferred_element_type=F32)


def _dot_nt(a, b):
    return lax.dot_general(a, b, (((1,), (1,)), ((), ())), preferred_element_type=F32)


def _mod_kernel(c_ref, w_ref, b_ref, o_ref):
    s = _silu(c_ref[...]).astype(BF16)
    o_ref[...] = _dot(s, w_ref[...]) + b_ref[...]


def _mod_call(c_all, ada_w16, ada_b):
    n, d3 = c_all.shape[0], ada_w16.shape[1]
    return pl.pallas_call(
        _mod_kernel,
        out_shape=jax.ShapeDtypeStruct((n, d3), F32),
        compiler_params=pltpu.CompilerParams(vmem_limit_bytes=VMEM_LIMIT),
        name="ada_mod",
    )(c_all, ada_w16, ada_b)


_C_SBQ, _C_SBK, _C_SBV, _C_SBG = 0, 512, 1024, 1536
_C_CQ, _C_CKV, _C_MLAG, _C_KR, _C_END = 2048, 2432, 2688, 3200, 3328
_HALF_ROPE = MLA_ROPE_DIM // 2


def _store_key_blocks_transposed(ref, v):
    tb = ref.shape[3]
    for i in range(ref.shape[1]):
        ref[0, i] = v[i * tb:(i + 1) * tb].T.astype(BF16)


def _proj_kernel(x_ref, mod_ref, pg_ref, w_ref, qg_ref, wa_ref, kvg_ref, wkn_ref, wv_ref,
                 cq_ref, s1_ref, s2_ref, ck_ref, sk_ref,
                 sbq_ref, sbk32_ref, sbk16_ref, sbv32_ref, sbvt_ref, gs_ref, qcat_ref,
                 ckv_ref, kcat_ref, mvt_ref, kr_ref, krp_ref):
    x = x_ref[0]
    rs = lax.rsqrt(jnp.mean(x * x, axis=-1, keepdims=True) + EPS)
    g = pg_ref[...] * (1.0 + mod_ref[0, 1:2, :])
    hb = ((x * rs) * g + mod_ref[0, 0:1, :]).astype(BF16)

    def proj(a, b):
        return _dot(hb, w_ref[:, a:b])

    sbq_ref[0] = (proj(_C_SBQ, _C_SBK) * (SB_HEAD_DIM ** -0.5 * LOG2E)).astype(BF16)
    k = proj(_C_SBK, _C_SBV)
    sbk32_ref[0] = k
    sbk16_ref[0] = k.astype(BF16)
    v = proj(_C_SBV, _C_SBG)
    sbv32_ref[0] = v
    _store_key_blocks_transposed(sbvt_ref, v)
    gs_ref[0, :, 0:512] = _silu(proj(_C_SBG, _C_CQ)).astype(BF16)
    gs_ref[0, :, 512:1024] = _silu(proj(_C_MLAG, _C_KR)).astype(BF16)

    cqn = _rms(proj(_C_CQ, _C_CKV), qg_ref[...]).astype(BF16)
    qa = _dot(cqn, wa_ref[...])
    cq_t, s1_t, s2_t = cq_ref[...], s1_ref[...], s2_ref[...]
    n_heads = qcat_ref.shape[2] // LANES
    for h in range(n_heads):
        a = qa[:, h * LANES:(h + 1) * LANES]
        up = pltpu.roll(a, _HALF_ROPE, 1)
        down = pltpu.roll(a, LANES - _HALF_ROPE, 1)
        qcat_ref[0, :, h * LANES:(h + 1) * LANES] = (a * cq_t + up * s1_t + down * s2_t).astype(BF16)

    ckvn = _rms(proj(_C_CKV, _C_MLAG), kvg_ref[...])
    ckv_ref[0] = ckvn
    cb = ckvn.astype(BF16)
    r = proj(_C_KR, _C_END)
    kro = r * ck_ref[...] + pltpu.roll(r, LANES - MLA_ROPE_DIM, 1) * sk_ref[...]
    kr_ref[0] = kro[:, :MLA_ROPE_DIM]
    krp_ref[0] = kro.astype(BF16)
    slot = pltpu.roll(kro, MLA_NOPE_DIM, 1)
    kn = _dot(cb, wkn_ref[...])
    for h in range(n_heads):
        kcat_ref[0, :, h * LANES:(h + 1) * LANES] = (kn[:, h * LANES:(h + 1) * LANES] + slot).astype(BF16)
    _store_key_blocks_transposed(mvt_ref, _dot(cb, wv_ref[...]))


def _proj_call(x, mod, pw, tabs, tm, tb):
    b, t, d = x.shape
    nt = t // tm
    row = lambda n: pl.BlockSpec((1, tm, n), lambda i, j: (i, j, 0))
    colt = pl.BlockSpec((1, tm // tb, 512, tb), lambda i, j: (i, j, 0, 0))
    full = lambda a: pl.BlockSpec(a.shape, lambda i, j: (0,) * a.ndim)
    tab = pl.BlockSpec((tm, LANES), lambda i, j: (j, 0))
    o = lambda n, dt: jax.ShapeDtypeStruct((b, t, n), dt)
    ot = jax.ShapeDtypeStruct((b, t // tb, 512, tb), BF16)
    return pl.pallas_call(
        _proj_kernel,
        grid=(b, nt),
        in_specs=[row(d), pl.BlockSpec((1, 3, d), lambda i, j: (i, 0, 0)), full(pw["pre_g"]),
                  full(pw["w_aug"]), full(pw["q_g"]), full(pw["w_a"]), full(pw["kv_g"]),
                  full(pw["w_kn"]), full(pw["w_v"]), tab, tab, tab, tab, tab],
        out_specs=[row(512), row(512), row(512), row(512), colt, row(1024), row(1024),
                   row(256), row(1024), colt, row(MLA_ROPE_DIM), row(LANES)],
        out_shape=[o(512, BF16), o(512, F32), o(512, BF16), o(512, F32), ot,
                   o(1024, BF16), o(1024, BF16), o(256, F32), o(1024, BF16), ot,
                   o(MLA_ROPE_DIM, F32), o(LANES, BF16)],
        compiler_params=_cparams(2),
        name="in_proj",
    )(x, mod, pw["pre_g"], pw["w_aug"], pw["q_g"], pw["w_a"], pw["kv_g"], pw["w_kn"], pw["w_v"],
      tabs["cq"], tabs["s1"], tabs["s2"], tabs["ck"], tabs["sk"])


def _emit_pipelined(tasks, lag):
    n_slots = max(j + (len(stages) - 1) * lag for j, stages in enumerate(tasks)) + 1
    for slot in range(n_slots):
        for j, stages in enumerate(tasks):
            s, rem = divmod(slot - j, lag)
            if slot >= j and rem == 0 and s < len(stages):
                stages[s]()


def _sb_task(state, c, get_q, get_k, get_vt, lmat, masked, skip_corner=False):
    t = {}

    def log_rem(z):
        low = jnp.minimum(z, 0.0)
        neg_relu = low - z
        l = jnp.log2(1.0 + jnp.exp2(low + neg_relu))
        return (neg_relu - l).astype(BF16)

    def scores():
        t["z"] = _dot_nt(get_k(), get_q())

    def logs():
        z = t.pop("z")
        if masked is None:
            t["z"] = [z]
            lr = log_rem(z)
        elif not skip_corner:
            z = jnp.where(masked, z, MASKED_SCORE)
            t["z"] = [z]
            lr = log_rem(z)
        else:
            h = z.shape[0] // 2
            top = jnp.where(masked[:h], z[:h], MASKED_SCORE)
            corner = jnp.where(masked[h:, h:], z[h:, h:], MASKED_SCORE)
            t["z"] = [top, corner]
            lr = jnp.concatenate(
                [log_rem(top), jnp.concatenate([jnp.zeros((h, h), BF16), log_rem(corner)], axis=1)],
                axis=0)
        t["tail"] = _dot(lmat, lr)

    def weights():
        tail = t.pop("tail")
        zs = t.pop("z")
        if len(zs) == 1:
            w = jnp.exp2(zs[0] + tail).astype(BF16)
        else:
            h = zs[0].shape[0]
            w_top = jnp.exp2(zs[0] + tail[:h]).astype(BF16)
            w_corner = jnp.exp2(zs[1] + tail[h:, h:]).astype(BF16)
            w = jnp.concatenate(
                [w_top, jnp.concatenate([jnp.zeros((h, h), BF16), w_corner], axis=1)], axis=0)
        r = state["r"][c]
        t["scale"] = jnp.exp2(r)
        state["r"][c] = r + tail[0:1, :]
        t["pv"] = _dot(get_vt(), w)

    def accumulate():
        state["acc"][c] = state["acc"][c] + t.pop("pv") * t.pop("scale")

    return [scores, logs, weights, accumulate]


def _split_heads(qp):
    lane = lax.broadcasted_iota(jnp.int32, qp.shape, 1)
    qf = qp.astype(F32)
    first = lane < HEAD
    return jnp.where(first, qf, 0.0).astype(BF16), jnp.where(first, 0.0, qf).astype(BF16)


def _causal_units(nt):
    return [(i, i - dist) for dist in range(nt) for i in range(dist, nt)]


def _sb_prompt_kernel(q_ref, k_ref, vt_ref, l_ref, o_ref, *, tb):
    nt = q_ref.shape[1] // tb
    key = lax.broadcasted_iota(jnp.int32, (tb, tb), 0)
    qry = lax.broadcasted_iota(jnp.int32, (tb, tb), 1)
    causal = key < qry
    lmat = l_ref[...]
    qs = {}

    def unit(i, j, h):
        def get_q():
            if (i, h) not in qs:
                qs[(i, 0)], qs[(i, 1)] = _split_heads(q_ref[0, i * tb:(i + 1) * tb, :])
            return qs[(i, h)]

        def write_tile():
            pair = jnp.concatenate([state["acc"][2 * i], state["acc"][2 * i + 1]], axis=0)
            o_ref[0, i * tb:(i + 1) * tb, :] = pair.T.astype(BF16)

        get_k = lambda: k_ref[0, j * tb:(j + 1) * tb, :]
        get_vt = lambda: vt_ref[0, j, h * HEAD:(h + 1) * HEAD, :]
        stages = _sb_task(state, 2 * i + h, get_q, get_k, get_vt, lmat,
                          causal if i == j else None, skip_corner=True)
        return stages + [write_tile] if (j == 0 and h == 1) else stages

    state = {"r": [jnp.zeros((1, tb), F32)] * (2 * nt), "acc": [jnp.zeros((HEAD, tb), F32)] * (2 * nt)}
    _emit_pipelined([unit(i, j, h) for i, j in _causal_units(nt) for h in range(2)], SB_LAG)


def _suffix_matrix(n):
    s = lax.broadcasted_iota(jnp.int32, (n, n), 0)
    j = lax.broadcasted_iota(jnp.int32, (n, n), 1)
    return (j >= s).astype(BF16)


def _sb_prompt_call(q, k, vt, tb):
    b, t, w = q.shape
    return pl.pallas_call(
        functools.partial(_sb_prompt_kernel, tb=tb),
        grid=(b, w // LANES),
        in_specs=[pl.BlockSpec((1, t, LANES), lambda i, h: (i, 0, h)),
                  pl.BlockSpec((1, t, LANES), lambda i, h: (i, 0, h)),
                  pl.BlockSpec((1, t // tb, LANES, tb), lambda i, h: (i, 0, h, 0)),
                  pl.BlockSpec((tb, tb), lambda i, h: (0, 0))],
        out_specs=pl.BlockSpec((1, t, LANES), lambda i, h: (i, 0, h)),
        out_shape=jax.ShapeDtypeStruct((b, t, w), BF16),
        compiler_params=_cparams(2),
        name="sb_prompt",
    )(q, k, vt, _suffix_matrix(tb))


def _pair_rows(q_lo, q_hi):
    return jnp.concatenate([q_lo, q_hi], axis=0)


def _pair_out(acc):
    acc_t = acc.T
    lane = lax.broadcasted_iota(jnp.int32, (HEAD, LANES), 1)
    return jnp.where(lane < HEAD, acc_t[:HEAD], acc_t[HEAD:])


def _sb_sample_kernel(q_ref, kn_ref, vtn_ref, ktp_ref, vtp_ref, ls_ref, l_ref, o_ref, *, tk):
    tq = q_ref.shape[1]
    n_past = ktp_ref.shape[2] // tk
    qrows = _pair_rows(*_split_heads(q_ref[0]))
    key = lax.broadcasted_iota(jnp.int32, (tq, 2 * tq), 0)
    qry = lax.broadcasted_iota(jnp.int32, (tq, 2 * tq), 1) % tq
    causal = key < qry
    lmat = l_ref[...]
    state = {"r": [jnp.zeros((1, 2 * tq), F32)], "acc": [jnp.zeros((LANES, 2 * tq), F32)]}
    get_q = lambda: qrows
    tasks = [_sb_task(state, 0, get_q, lambda: kn_ref[0], lambda: vtn_ref[0, 0], ls_ref[...], causal)]
    for kb in reversed(range(n_past)):
        get_k = lambda kb=kb: ktp_ref[0, :, kb * tk:(kb + 1) * tk].T.astype(BF16)
        get_vt = lambda kb=kb: vtp_ref[0, :, kb * tk:(kb + 1) * tk].astype(BF16)
        tasks.append(_sb_task(state, 0, get_q, get_k, get_vt, lmat, None))
    _emit_pipelined(tasks, SB_SAMPLE_LAG)
    o_ref[0] = _pair_out(state["acc"][0]).astype(BF16)


def _sb_sample_call(q, kn, vtn, ktp, vtp, tk):
    b, t, w = q.shape
    assert 2 * t == LANES
    tp = ktp.shape[2]
    new = pl.BlockSpec((1, t, LANES), lambda i, h: (i, 0, h))
    past = pl.BlockSpec((1, LANES, tp), lambda i, h: (i, h, 0))
    return pl.pallas_call(
        functools.partial(_sb_sample_kernel, tk=tk),
        grid=(b, w // LANES),
        in_specs=[new, new, pl.BlockSpec((1, 1, LANES, t), lambda i, h: (i, 0, h, 0)), past, past,
                  pl.BlockSpec((t, t), lambda i, h: (0, 0)),
                  pl.BlockSpec((tk, tk), lambda i, h: (0, 0))],
        out_specs=new,
        out_shape=jax.ShapeDtypeStruct((b, t, w), BF16),
        compiler_params=_cparams(2),
        name="sb_sample",
    )(q, kn, vtn, ktp, vtp, _suffix_matrix(t), _suffix_matrix(tk))


def _mla_task(state, c, get_q, get_k, get_vt, masked, skip_corner=False):
    t = {}

    def scores():
        t["s"] = _dot_nt(get_k(), get_q())

    def probs():
        s = t.pop("s")
        m = state["m"][c]
        if masked is None or not skip_corner:
            if masked is not None:
                s = jnp.where(masked, s, MASKED_SCORE)
            m_new = jnp.maximum(m, jnp.max(s, axis=0, keepdims=True))
            p = jnp.exp2(s - m_new).astype(BF16)
        else:
            h = s.shape[0] // 2
            top = jnp.where(masked[:h], s[:h], MASKED_SCORE)
            corner = jnp.where(masked[h:, h:], s[h:, h:], MASKED_SCORE)
            top_max = jnp.max(top, axis=0, keepdims=True)
            corner_max = jnp.max(corner, axis=0, keepdims=True)
            tile_max = jnp.concatenate(
                [top_max[:, :h], jnp.maximum(top_max[:, h:], corner_max)], axis=1)
            m_new = jnp.maximum(m, tile_max)
            p_top = jnp.exp2(top - m_new).astype(BF16)
            p_corner = jnp.exp2(corner - m_new[:, h:]).astype(BF16)
            p = jnp.concatenate(
                [p_top, jnp.concatenate([jnp.zeros((h, h), BF16), p_corner], axis=1)], axis=0)
        t["alpha"] = jnp.exp2(m - m_new)
        state["m"][c] = m_new
        t["pv"] = _dot(_with_sum_row(get_vt()), p)

    def accumulate():
        state["acc"][c] = t.pop("alpha") * state["acc"][c] + t.pop("pv")

    return [scores, probs, accumulate]


SUM_ROWS = 16


def _with_sum_row(vt):
    row = lax.broadcasted_iota(jnp.int32, (SUM_ROWS, vt.shape[1]), 0)
    return jnp.concatenate([vt, (row == 0).astype(vt.dtype)], axis=0)


def _mla_state(n, nd, nq):
    return {"m": [jnp.full((1, nq), MASKED_SCORE, F32)] * n,
            "acc": [jnp.zeros((nd + SUM_ROWS, nq), F32)] * n}


def _mla_prompt_kernel(q_ref, k_ref, vt_ref, o_ref, *, tb):
    nt = q_ref.shape[1] // tb
    key = lax.broadcasted_iota(jnp.int32, (tb, tb), 0)
    qry = lax.broadcasted_iota(jnp.int32, (tb, tb), 1)
    visible = (key // CHUNK) <= (qry // CHUNK)
    state = _mla_state(2 * nt, HEAD, tb)

    def unit(i, j, h):
        def write_tile():
            accs = [state["acc"][2 * i + hh] for hh in range(2)]
            outs = [a[:HEAD] * (1.0 / a[HEAD:HEAD + 1]) for a in accs]
            o_ref[0, i * tb:(i + 1) * tb, :] = jnp.concatenate(outs, axis=0).T.astype(BF16)

        q = lambda: q_ref[0, i * tb:(i + 1) * tb, h * LANES:(h + 1) * LANES]
        get_k = lambda: k_ref[0, j * tb:(j + 1) * tb, h * LANES:(h + 1) * LANES]
        get_vt = lambda: vt_ref[0, j, h * HEAD:(h + 1) * HEAD, :]
        stages = _mla_task(state, 2 * i + h, q, get_k, get_vt, visible if i == j else None,
                           skip_corner=True)
        return stages + [write_tile] if (j == 0 and h == 1) else stages

    _emit_pipelined([unit(i, j, h) for i, j in _causal_units(nt) for h in range(2)], MLA_LAG)


def _mla_prompt_call(q, k, vt, tb):
    b, t, w2 = q.shape
    w = w2 // 2
    return pl.pallas_call(
        functools.partial(_mla_prompt_kernel, tb=tb),
        grid=(b, w // LANES),
        in_specs=[pl.BlockSpec((1, t, 2 * LANES), lambda i, h: (i, 0, h)),
                  pl.BlockSpec((1, t, 2 * LANES), lambda i, h: (i, 0, h)),
                  pl.BlockSpec((1, t // tb, LANES, tb), lambda i, h: (i, 0, h, 0))],
        out_specs=pl.BlockSpec((1, t, LANES), lambda i, h: (i, 0, h)),
        out_shape=jax.ShapeDtypeStruct((b, t, w), BF16),
        compiler_params=_cparams(2),
        name="mla_prompt",
    )(q, k, vt)


def _mla_sample_kernel(q_ref, cn_ref, krn_ref, cp_ref, krtp_ref, wknt_ref, sel_ref, wvf_ref, o_ref,
                       *, tk):
    tq = q_ref.shape[1]
    n_heads = wknt_ref.shape[0]
    n_past = cp_ref.shape[1] // tk
    slots = [q_ref[0, :, h * LANES:(h + 1) * LANES] for h in range(n_heads)]
    qa = jnp.concatenate([_dot(s, wknt_ref[h]) for h, s in enumerate(slots)], axis=0).astype(BF16)
    qr = jnp.concatenate([_dot(s, sel_ref[...]) for s in slots], axis=0).astype(BF16)
    nq = n_heads * tq
    state = {"m": jnp.full((nq, 1), MASKED_SCORE, F32), "l": jnp.zeros((nq, 1), F32),
             "acc": jnp.zeros((nq, qa.shape[1]), F32)}

    def unit(get_latent, rope_scores):
        t = {}

        def scores():
            t["lat"] = get_latent()
            t["s"] = _dot_nt(qa, t["lat"]) + rope_scores()

        def probs():
            s = t.pop("s")
            m = state["m"]
            m_new = jnp.maximum(m, jnp.max(s, axis=1, keepdims=True))
            p = jnp.exp2(s - m_new)
            alpha = jnp.exp2(m - m_new)
            state["m"] = m_new
            state["l"] = alpha * state["l"] + jnp.sum(p, axis=1, keepdims=True)
            t["alpha"] = alpha
            t["pv"] = _dot(p.astype(BF16), t.pop("lat"))

        def accumulate():
            state["acc"] = t.pop("alpha") * state["acc"] + t.pop("pv")

        return [scores, probs, accumulate]

    tasks = [unit(lambda: cn_ref[0].astype(BF16), lambda: _dot_nt(qr, krn_ref[0]))]
    pad = jnp.zeros((LANES - MLA_ROPE_DIM, tk), BF16)
    for kb in range(n_past):
        get_latent = lambda kb=kb: cp_ref[0, kb * tk:(kb + 1) * tk, :].astype(BF16)
        rope_scores = lambda kb=kb: _dot(qr, jnp.concatenate(
            [krtp_ref[0, :, kb * tk:(kb + 1) * tk].astype(BF16), pad], axis=0))
        tasks.append(unit(get_latent, rope_scores))
    _emit_pipelined(tasks, MLA_SAMPLE_LAG)
    lat_out = (state["acc"] * (1.0 / state["l"])).astype(BF16)
    out = _dot(lat_out[0:tq], wvf_ref[0])
    for h in range(1, n_heads):
        out = out + _dot(lat_out[h * tq:(h + 1) * tq], wvf_ref[h])
    o_ref[0] = out.astype(BF16)


def _mla_sample_call(q, ckv_new, kr_new, ckv_past, krt_past, pw, tk, past_len):
    b, t, w = q.shape
    tp, r_kv = ckv_past.shape[1], ckv_past.shape[2]
    assert past_len % CHUNK == 0 and t <= CHUNK
    full = lambda a: pl.BlockSpec(a.shape, lambda i: (0,) * a.ndim)
    return pl.pallas_call(
        functools.partial(_mla_sample_kernel, tk=tk),
        grid=(b,),
        in_specs=[pl.BlockSpec((1, t, w), lambda i: (i, 0, 0)),
                  pl.BlockSpec((1, t, r_kv), lambda i: (i, 0, 0)),
                  pl.BlockSpec((1, t, LANES), lambda i: (i, 0, 0)),
                  pl.BlockSpec((1, tp, r_kv), lambda i: (i, 0, 0)),
                  pl.BlockSpec((1, MLA_ROPE_DIM, tp), lambda i: (i, 0, 0)),
                  full(pw["w_knt"]), full(pw["rope_sel"]), full(pw["w_vf"])],
        out_specs=pl.BlockSpec((1, t, pw["w_vf"].shape[2]), lambda i: (i, 0, 0)),
        out_shape=jax.ShapeDtypeStruct((b, t, pw["w_vf"].shape[2]), BF16),
        compiler_params=_cparams(1),
        name="mla_sample",
    )(q, ckv_new, kr_new, ckv_past, krt_past, pw["w_knt"], pw["rope_sel"], pw["w_vf"])


def _out_kernel(sbo_ref, mlao_ref, gs_ref, x_ref, mod_ref, w_ref, g_ref, y_ref):
    half = sbo_ref.shape[2]
    m0 = (sbo_ref[0].astype(F32) * gs_ref[0, :, :half].astype(F32)).astype(BF16)
    m1 = (mlao_ref[0].astype(F32) * gs_ref[0, :, half:].astype(F32)).astype(BF16)
    out = _dot(m0, w_ref[:half, :]) + _dot(m1, w_ref[half:, :])
    y_ref[0] = x_ref[0] + mod_ref[0, 2:3, :] * _rms(out, g_ref[...])


def _out_call(sbo, mlao, gs, x, mod, w_out16, post_g, tm):
    b, t, d = x.shape
    row = lambda n: pl.BlockSpec((1, tm, n), lambda i, j: (i, j, 0))
    full = lambda a: pl.BlockSpec(a.shape, lambda i, j: (0,) * a.ndim)
    return pl.pallas_call(
        _out_kernel,
        grid=(b, t // tm),
        in_specs=[row(512), row(512), row(1024), row(d),
                  pl.BlockSpec((1, 3, d), lambda i, j: (i, 0, 0)), full(w_out16), full(post_g)],
        out_specs=row(d),
        out_shape=jax.ShapeDtypeStruct((b, t, d), F32),
        compiler_params=_cparams(2),
        name="out_proj",
    )(sbo, mlao, gs, x, mod, w_out16, post_g)


def _prep_weights(pre_g, w_in, q_g, w_uq, kv_g, w_ukv):
    d = w_in.shape[0]
    n_heads = w_ukv.shape[1] // (MLA_NOPE_DIM + MLA_V_DIM)
    half = MLA_ROPE_DIM // 2
    sizes = (512, 512, 512, 512, w_uq.shape[0], w_ukv.shape[0], MLA_ROPE_DIM, 512)
    starts = [0]
    for s in sizes:
        starts.append(starts[-1] + s)
    cols = lambda i: w_in[:, starts[i]:starts[i + 1]]
    kr = cols(6)
    kr_swapped = jnp.concatenate([-kr[:, half:], kr[:, :half]], axis=1)
    zpad = jnp.zeros((d, LANES - 2 * MLA_ROPE_DIM), w_in.dtype)
    w_aug = jnp.concatenate([cols(0), cols(1), cols(2), cols(3), cols(4), cols(5), cols(7),
                             kr, kr_swapped, zpad], axis=1).astype(BF16)
    assert w_aug.shape[1] == _C_END

    r_q = w_uq.shape[0]
    uq = w_uq.reshape(r_q, n_heads, MLA_NOPE_DIM + MLA_ROPE_DIM)
    zq = jnp.zeros((r_q, n_heads, LANES - MLA_NOPE_DIM - MLA_ROPE_DIM), w_uq.dtype)
    w_a = jnp.concatenate([uq, zq], axis=-1).reshape(r_q, n_heads * LANES).astype(BF16)

    r_kv = w_ukv.shape[0]
    ukv = w_ukv.reshape(r_kv, n_heads, MLA_NOPE_DIM + MLA_V_DIM)
    uk, uv = ukv[..., :MLA_NOPE_DIM], ukv[..., MLA_NOPE_DIM:]
    zk = jnp.zeros((r_kv, n_heads, LANES - MLA_NOPE_DIM), w_ukv.dtype)
    w_kn = jnp.concatenate([uk, zk], axis=-1).reshape(r_kv, n_heads * LANES).astype(BF16)
    w_v = uv.reshape(r_kv, n_heads * MLA_V_DIM).astype(BF16)

    w_knt = jnp.transpose(jnp.concatenate([uk, zk], axis=-1), (1, 2, 0)).astype(BF16)
    eye = jnp.eye(n_heads, dtype=w_ukv.dtype)
    w_vf = jnp.einsum("rhd,hg->hrgd", uv, eye).reshape(n_heads, r_kv, n_heads * MLA_V_DIM).astype(BF16)
    src = lax.broadcasted_iota(jnp.int32, (LANES, LANES), 0)
    dst = lax.broadcasted_iota(jnp.int32, (LANES, LANES), 1)
    rope_sel = ((src == dst + MLA_NOPE_DIM) & (dst < MLA_ROPE_DIM)).astype(BF16)

    return dict(pre_g=pre_g[None, :], w_aug=w_aug, q_g=q_g[None, :], w_a=w_a,
                kv_g=kv_g[None, :], w_kn=w_kn, w_v=w_v, w_knt=w_knt, w_vf=w_vf, rope_sel=rope_sel)


def _rope_tables(pos):
    r = MLA_ROPE_DIM
    half = r // 2
    inv_freq = ROPE_THETA ** (-jnp.arange(0, r, 2, dtype=F32) / r)
    ang = pos.astype(F32)[:, None] * inv_freq[None, :]
    cos, sin = jnp.cos(ang), jnp.sin(ang)
    t = pos.shape[0]
    scale = (MLA_NOPE_DIM + MLA_ROPE_DIM) ** -0.5 * LOG2E
    z = lambda n: jnp.zeros((t, n), F32)
    cq = jnp.concatenate([jnp.full((t, MLA_NOPE_DIM), scale, F32), cos * scale, cos * scale,
                          z(LANES - MLA_NOPE_DIM - r)], axis=1)
    s1 = jnp.concatenate([z(MLA_NOPE_DIM + half), sin * scale, z(LANES - MLA_NOPE_DIM - r)], axis=1)
    s2 = jnp.concatenate([z(MLA_NOPE_DIM), -sin * scale, z(LANES - MLA_NOPE_DIM - half)], axis=1)
    ck = jnp.concatenate([cos, cos, z(LANES - r)], axis=1)
    sk = jnp.concatenate([sin, sin, z(LANES - r)], axis=1)
    return dict(cq=cq, s1=s1, s2=s2, ck=ck, sk=sk)


PROMPT_T = 256
PROJ_TM = 512
OUT_TM = 512
PAST_TK = 256
SB_LAG = 2
MLA_LAG = 6
SB_SAMPLE_LAG = 4
MLA_SAMPLE_LAG = 3


def kernel(x_prompt, x_sample, cache_sb_k, cache_sb_v, cache_mla_ckv, cache_mla_krope, c_prompt, c_sample, ada_w, ada_b, pre_norm_g, w_in, q_norm_g, w_uq, kv_norm_g, w_ukv, w_out, post_norm_g):
    depth = ada_w.shape[0]
    bp, tp, d = x_prompt.shape
    bs, ts, _ = x_sample.shape
    past_len = cache_sb_k.shape[2]
    n_sb = cache_sb_k.shape[3]

    tabs_p = _rope_tables(jnp.arange(tp, dtype=jnp.int32))
    tabs_s = _rope_tables(past_len + jnp.arange(ts, dtype=jnp.int32))
    c_all = jnp.concatenate([c_prompt, c_sample], axis=0)

    yp, ys = x_prompt, x_sample
    new_p, new_s = [], []
    for l in range(depth):
        pw = _prep_weights(pre_norm_g[l], w_in[l], q_norm_g[l], w_uq[l], kv_norm_g[l], w_ukv[l])
        w_out16 = w_out[l].astype(BF16)
        post_g = post_norm_g[l][None, :]
        mod = _mod_call(c_all, ada_w[l].astype(BF16), ada_b[l][None, :])
        mod_p = mod[:bp].reshape(bp, 3, d)
        mod_s = mod[bp:].reshape(bs, 3, d)

        (sbq, sbk32, sbk16, sbv32, sbvt, gs, qcat, ckv, kcat, mvt, kr, _) = _proj_call(
            yp, mod_p, pw, tabs_p, PROJ_TM, PROMPT_T)
        sbo = _sb_prompt_call(sbq, sbk16, sbvt, PROMPT_T)
        mlao = _mla_prompt_call(qcat, kcat, mvt, PROMPT_T)
        yp = _out_call(sbo, mlao, gs, yp, mod_p, w_out16, post_g, OUT_TM)
        new_p.append((sbk32.reshape(bp, tp, n_sb, SB_HEAD_DIM), sbv32.reshape(bp, tp, n_sb, SB_HEAD_DIM),
                      ckv, kr))

        (sbq, sbk32, sbk16, sbv32, sbvt, gs, qcat, ckv, _, _, kr, krp) = _proj_call(
            ys, mod_s, pw, tabs_s, ts, ts)
        ktp = jnp.transpose(cache_sb_k[l], (0, 2, 3, 1)).reshape(bs, n_sb * SB_HEAD_DIM, past_len)
        vtp = jnp.transpose(cache_sb_v[l], (0, 2, 3, 1)).reshape(bs, n_sb * SB_HEAD_DIM, past_len)
        sbo = _sb_sample_call(sbq, sbk16, sbvt, ktp, vtp, PAST_TK)
        krt_p = jnp.transpose(cache_mla_krope[l], (0, 2, 1))
        mlao = _mla_sample_call(qcat, ckv, krp, cache_mla_ckv[l], krt_p, pw, PAST_TK, past_len)
        ys = _out_call(sbo, mlao, gs, ys, mod_s, w_out16, post_g, ts)
        new_s.append((sbk32.reshape(bs, ts, n_sb, SB_HEAD_DIM), sbv32.reshape(bs, ts, n_sb, SB_HEAD_DIM),
                      ckv, kr))

    stack = lambda items, i: items[0][i][None] if depth == 1 else jnp.stack([it[i] for it in items])
    return (yp, ys,
            stack(new_p, 0), stack(new_p, 1), stack(new_p, 2), stack(new_p, 3),
            stack(new_s, 0), stack(new_s, 1), stack(new_s, 2), stack(new_s, 3))
```

```python
import functools

import jax
import jax.numpy as jnp
from jax import lax
from jax.experimental import pallas as pl
from jax.experimental.pallas import tpu as pltpu

F32 = jnp.float32
BF16 = jnp.bfloat16

CHUNK = 64
SB_HEAD_DIM = 64
MLA_V_DIM = 64
MLA_NOPE_DIM = 64
MLA_ROPE_DIM = 32
ROPE_THETA = 10000.0
EPS = 1e-6
LOG2E = 1.4426950408889634
LANES = 128
HEAD = 64
MASKED_SCORE = -1e30

VMEM_LIMIT = 56 * 1024 * 1024


def _cparams(n_grid, flags=None):
    return pltpu.CompilerParams(dimension_semantics=("arbitrary",) * n_grid,
                                vmem_limit_bytes=VMEM_LIMIT, flags=flags)


def _silu(x):
    return x * (1.0 / (1.0 + jnp.exp(-x)))


def _rms(x, g):
    return x * lax.rsqrt(jnp.mean(x * x, axis=-1, keepdims=True) + EPS) * g


def _dot(a, b):
    return jnp.dot(a, b, preferred_element_type=F32)


def _dot_nt(a, b):
    return lax.dot_general(a, b, (((1,), (1,)), ((), ())), preferred_element_type=F32)


def _mod_kernel(c_ref, w_ref, b_ref, o_ref):
    s = _silu(c_ref[...]).astype(BF16)
    o_ref[...] = _dot(s, w_ref[...]) + b_ref[...]


def _mod_call(c_all, ada_w16, ada_b):
    n, d3 = c_all.shape[0], ada_w16.shape[1]
    return pl.pallas_call(
        _mod_kernel,
        out_shape=jax.ShapeDtypeStruct((n, d3), F32),
        compiler_params=pltpu.CompilerParams(vmem_limit_bytes=VMEM_LIMIT),
        name="ada_mod",
    )(c_all, ada_w16, ada_b)


_C_SBQ, _C_SBK, _C_SBV, _C_SBG = 0, 512, 1024, 1536
_C_CQ, _C_CKV, _C_MLAG, _C_KR, _C_END = 2048, 2432, 2688, 3200, 3328
_HALF_ROPE = MLA_ROPE_DIM // 2


def _store_key_blocks_transposed(ref, v):
    tb = ref.shape[3]
    for i in range(ref.shape[1]):
        ref[0, i] = v[i * tb:(i + 1) * tb].T.astype(BF16)


def _proj_kernel(x_ref, mod_ref, pg_ref, w_ref, qg_ref, wa_ref, kvg_ref, wkn_ref, wv_ref,
                 cq_ref, s1_ref, s2_ref, ck_ref, sk_ref,
                 sbq_ref, sbk32_ref, sbk16_ref, sbv32_ref, sbvt_ref, gs_ref, qcat_ref,
                 ckv_ref, kcat_ref, mvt_ref, kr_ref, krp_ref):
    x = x_ref[0]
    rs = lax.rsqrt(jnp.mean(x * x, axis=-1, keepdims=True) + EPS)
    g = pg_ref[...] * (1.0 + mod_ref[0, 1:2, :])
    hb = ((x * rs) * g + mod_ref[0, 0:1, :]).astype(BF16)

    def proj(a, b):
        return _dot(hb, w_ref[:, a:b])

    sbq_ref[0] = (proj(_C_SBQ, _C_SBK) * (SB_HEAD_DIM ** -0.5 * LOG2E)).astype(BF16)
    k = proj(_C_SBK, _C_SBV)
    sbk32_ref[0] = k
    sbk16_ref[0] = k.astype(BF16)
    v = proj(_C_SBV, _C_SBG)
    sbv32_ref[0] = v
    _store_key_blocks_transposed(sbvt_ref, v)
    gs_ref[0, :, 0:512] = _silu(proj(_C_SBG, _C_CQ)).astype(BF16)
    gs_ref[0, :, 512:1024] = _silu(proj(_C_MLAG, _C_KR)).astype(BF16)

    cqn = _rms(proj(_C_CQ, _C_CKV), qg_ref[...]).astype(BF16)
    qa = _dot(cqn, wa_ref[...])
    cq_t, s1_t, s2_t = cq_ref[...], s1_ref[...], s2_ref[...]
    n_heads = qcat_ref.shape[2] // LANES
    for h in range(n_heads):
        a = qa[:, h * LANES:(h + 1) * LANES]
        up = pltpu.roll(a, _HALF_ROPE, 1)
        down = pltpu.roll(a, LANES - _HALF_ROPE, 1)
        qcat_ref[0, :, h * LANES:(h + 1) * LANES] = (a * cq_t + up * s1_t + down * s2_t).astype(BF16)

    ckvn = _rms(proj(_C_CKV, _C_MLAG), kvg_ref[...])
    ckv_ref[0] = ckvn
    cb = ckvn.astype(BF16)
    r = proj(_C_KR, _C_END)
    kro = r * ck_ref[...] + pltpu.roll(r, LANES - MLA_ROPE_DIM, 1) * sk_ref[...]
    kr_ref[0] = kro[:, :MLA_ROPE_DIM]
    krp_ref[0] = kro.astype(BF16)
    slot = pltpu.roll(kro, MLA_NOPE_DIM, 1)
    kn = _dot(cb, wkn_ref[...])
    for h in range(n_heads):
        kcat_ref[0, :, h * LANES:(h + 1) * LANES] = (kn[:, h * LANES:(h + 1) * LANES] + slot).astype(BF16)
    _store_key_blocks_transposed(mvt_ref, _dot(cb, wv_ref[...]))


def _proj_call(x, mod, pw, tabs, tm, tb):
    b, t, d = x.shape
    nt = t // tm
    row = lambda n: pl.BlockSpec((1, tm, n), lambda i, j: (i, j, 0))
    colt = pl.BlockSpec((1, tm // tb, 512, tb), lambda i, j: (i, j, 0, 0))
    full = lambda a: pl.BlockSpec(a.shape, lambda i, j: (0,) * a.ndim)
    tab = pl.BlockSpec((tm, LANES), lambda i, j: (j, 0))
    o = lambda n, dt: jax.ShapeDtypeStruct((b, t, n), dt)
    ot = jax.ShapeDtypeStruct((b, t // tb, 512, tb), BF16)
    return pl.pallas_call(
        _proj_kernel,
        grid=(b, nt),
        in_specs=[row(d), pl.BlockSpec((1, 3, d), lambda i, j: (i, 0, 0)), full(pw["pre_g"]),
                  full(pw["w_aug"]), full(pw["q_g"]), full(pw["w_a"]), full(pw["kv_g"]),
                  full(pw["w_kn"]), full(pw["w_v"]), tab, tab, tab, tab, tab],
        out_specs=[row(512), row(512), row(512), row(512), colt, row(1024), row(1024),
                   row(256), row(1024), colt, row(MLA_ROPE_DIM), row(LANES)],
        out_shape=[o(512, BF16), o(512, F32), o(512, BF16), o(512, F32), ot,
                   o(1024, BF16), o(1024, BF16), o(256, F32), o(1024, BF16), ot,
                   o(MLA_ROPE_DIM, F32), o(LANES, BF16)],
        compiler_params=_cparams(2),
        name="in_proj",
    )(x, mod, pw["pre_g"], pw["w_aug"], pw["q_g"], pw["w_a"], pw["kv_g"], pw["w_kn"], pw["w_v"],
      tabs["cq"], tabs["s1"], tabs["s2"], tabs["ck"], tabs["sk"])


def _emit_pipelined(tasks, lag):
    n_slots = max(j + (len(stages) - 1) * lag for j, stages in enumerate(tasks)) + 1
    for slot in range(n_slots):
        for j, stages in enumerate(tasks):
            s, rem = divmod(slot - j, lag)
            if slot >= j and rem == 0 and s < len(stages):
                stages[s]()


def _sb_task(state, c, get_q, get_k, get_vt, lmat, masked, skip_corner=False):
    t = {}

    def log_rem(z):
        low = jnp.minimum(z, 0.0)
        neg_relu = low - z
        l = jnp.log2(1.0 + jnp.exp2(low + neg_relu))
        return (neg_relu - l).astype(BF16)

    def scores():
        t["z"] = _dot_nt(get_k(), get_q())

    def logs():
        z = t.pop("z")
        if masked is None:
            t["z"] = [z]
            lr = log_rem(z)
        elif not skip_corner:
            z = jnp.where(masked, z, MASKED_SCORE)
            t["z"] = [z]
            lr = log_rem(z)
        else:
            h = z.shape[0] // 2
            top = jnp.where(masked[:h], z[:h], MASKED_SCORE)
            corner = jnp.where(masked[h:, h:], z[h:, h:], MASKED_SCORE)
            t["z"] = [top, corner]
            lr = jnp.concatenate(
                [log_rem(top), jnp.concatenate([jnp.zeros((h, h), BF16), log_rem(corner)], axis=1)],
                axis=0)
        t["tail"] = _dot(lmat, lr)

    def weights():
        tail = t.pop("tail")
        zs = t.pop("z")
        if len(zs) == 1:
            w = jnp.exp2(zs[0] + tail).astype(BF16)
        else:
            h = zs[0].shape[0]
            w_top = jnp.exp2(zs[0] + tail[:h]).astype(BF16)
            w_corner = jnp.exp2(zs[1] + tail[h:, h:]).astype(BF16)
            w = jnp.concatenate(
                [w_top, jnp.concatenate([jnp.zeros((h, h), BF16), w_corner], axis=1)], axis=0)
        r = state["r"][c]
        t["scale"] = jnp.exp2(r)
        state["r"][c] = r + tail[0:1, :]
        t["pv"] = _dot(get_vt(), w)

    def accumulate():
        state["acc"][c] = state["acc"][c] + t.pop("pv") * t.pop("scale")

    return [scores, logs, weights, accumulate]


def _split_heads(qp):
    lane = lax.broadcasted_iota(jnp.int32, qp.shape, 1)
    qf = qp.astype(F32)
    first = lane < HEAD
    return jnp.where(first, qf, 0.0).astype(BF16), jnp.where(first, 0.0, qf).astype(BF16)


def _causal_units(nt):
    return [(i, i - dist) for dist in range(nt) for i in range(dist, nt)]


def _store_gated(o_ref, g_ref, rows, dims_by_query):
    o_ref[0, rows, :] = (dims_by_query.T * g_ref[0, rows, :].astype(F32)).astype(BF16)


def _sb_prompt_kernel(q_ref, k_ref, vt_ref, l_ref, g_ref, o_ref, *, tb):
    nt = q_ref.shape[1] // tb
    key = lax.broadcasted_iota(jnp.int32, (tb, tb), 0)
    qry = lax.broadcasted_iota(jnp.int32, (tb, tb), 1)
    causal = key < qry
    lmat = l_ref[...]
    state = {"r": [jnp.zeros((1, tb), F32)] * (2 * nt), "acc": [jnp.zeros((HEAD, tb), F32)] * (2 * nt)}
    qs = {}

    def unit(i, j, h):
        def get_q():
            if (i, h) not in qs:
                qs[(i, 0)], qs[(i, 1)] = _split_heads(q_ref[0, i * tb:(i + 1) * tb, :])
            return qs[(i, h)]

        def write_tile():
            pair = jnp.concatenate([state["acc"][2 * i], state["acc"][2 * i + 1]], axis=0)
            _store_gated(o_ref, g_ref, slice(i * tb, (i + 1) * tb), pair)

        get_k = lambda: k_ref[0, j * tb:(j + 1) * tb, :]
        get_vt = lambda: vt_ref[0, j, h * HEAD:(h + 1) * HEAD, :]
        stages = _sb_task(state, 2 * i + h, get_q, get_k, get_vt, lmat,
                          causal if i == j else None, skip_corner=True)
        return stages + [write_tile] if (j == 0 and h == 1) else stages

    _emit_pipelined([unit(i, j, h) for i, j in _causal_units(nt) for h in range(2)], SB_LAG)


def _mla_prompt_kernel(q_ref, k_ref, vt_ref, g_ref, o_ref, *, tb):
    nt = q_ref.shape[1] // tb
    key = lax.broadcasted_iota(jnp.int32, (tb, tb), 0)
    qry = lax.broadcasted_iota(jnp.int32, (tb, tb), 1)
    visible = (key // CHUNK) <= (qry // CHUNK)
    state = _mla_state(2 * nt, HEAD, tb)

    def unit(i, j, h):
        def write_tile():
            accs = [state["acc"][2 * i + hh] for hh in range(2)]
            outs = [a[:HEAD] * (1.0 / a[HEAD:HEAD + 1]) for a in accs]
            _store_gated(o_ref, g_ref, slice(i * tb, (i + 1) * tb), jnp.concatenate(outs, axis=0))

        get_q = lambda: q_ref[0, i * tb:(i + 1) * tb, h * LANES:(h + 1) * LANES]
        get_k = lambda: k_ref[0, j * tb:(j + 1) * tb, h * LANES:(h + 1) * LANES]
        get_vt = lambda: vt_ref[0, j, h * HEAD:(h + 1) * HEAD, :]
        stages = _mla_task(state, 2 * i + h, get_q, get_k, get_vt,
                           visible if i == j else None, skip_corner=True)
        return stages + [write_tile] if (j == 0 and h == 1) else stages

    _emit_pipelined([unit(i, j, h) for i, j in _causal_units(nt) for h in range(2)], MLA_LAG)


def _suffix_matrix(n):
    s = lax.broadcasted_iota(jnp.int32, (n, n), 0)
    j = lax.broadcasted_iota(jnp.int32, (n, n), 1)
    return (j >= s).astype(BF16)


def _sb_prompt_call(q, k, vt, gs, tb):
    b, t, w = q.shape
    pair = pl.BlockSpec((1, t, LANES), lambda i, h: (i, 0, h))
    return pl.pallas_call(
        functools.partial(_sb_prompt_kernel, tb=tb),
        grid=(b, w // LANES),
        in_specs=[pair, pair, pl.BlockSpec((1, t // tb, LANES, tb), lambda i, h: (i, 0, h, 0)),
                  pl.BlockSpec((tb, tb), lambda i, h: (0, 0)), pair],
        out_specs=pair,
        out_shape=jax.ShapeDtypeStruct((b, t, w), BF16),
        compiler_params=_cparams(2),
        name="sb_prompt",
    )(q, k, vt, _suffix_matrix(tb), gs)


def _mla_prompt_call(q, k, vt, gs, tb):
    b, t, w2 = q.shape
    w = w2 // 2
    n_pairs = w // LANES
    slots = pl.BlockSpec((1, t, 2 * LANES), lambda i, h: (i, 0, h))
    return pl.pallas_call(
        functools.partial(_mla_prompt_kernel, tb=tb),
        grid=(b, n_pairs),
        in_specs=[slots, slots, pl.BlockSpec((1, t // tb, LANES, tb), lambda i, h: (i, 0, h, 0)),
                  pl.BlockSpec((1, t, LANES), lambda i, h: (i, 0, n_pairs + h))],
        out_specs=pl.BlockSpec((1, t, LANES), lambda i, h: (i, 0, h)),
        out_shape=jax.ShapeDtypeStruct((b, t, w), BF16),
        compiler_params=_cparams(2),
        name="mla_prompt",
    )(q, k, vt, gs)


def _group_rows(qg, n):
    lane = lax.broadcasted_iota(jnp.int32, qg.shape, 1)
    qf = qg.astype(F32)
    return jnp.concatenate(
        [jnp.where(lane // HEAD == h, qf, 0.0).astype(BF16) for h in range(n)], axis=0)


def _group_out(acc, n, t):
    acc_t = acc.T
    lane = lax.broadcasted_iota(jnp.int32, (t, acc.shape[0]), 1)
    out = acc_t[:t]
    for h in range(1, n):
        out = jnp.where(lane // HEAD == h, acc_t[h * t:(h + 1) * t], out)
    return out


def _sb_sample_kernel(q_ref, kn_ref, vtn_ref, ktp_ref, vtp_ref, ls_ref, l_ref, g_ref, o_ref, *, tk):
    tq, width = q_ref.shape[1], q_ref.shape[2]
    n = width // HEAD
    n_past = ktp_ref.shape[2] // tk
    qrows = _group_rows(q_ref[0], n)
    key = lax.broadcasted_iota(jnp.int32, (tq, n * tq), 0)
    qry = lax.broadcasted_iota(jnp.int32, (tq, n * tq), 1) % tq
    causal = key < qry
    lmat = l_ref[...]
    state = {"r": [jnp.zeros((1, n * tq), F32)], "acc": [jnp.zeros((width, n * tq), F32)]}
    get_q = lambda: qrows
    tasks = [_sb_task(state, 0, get_q, lambda: kn_ref[0], lambda: vtn_ref[0, 0], ls_ref[...], causal)]
    for kb in reversed(range(n_past)):
        get_k = lambda kb=kb: ktp_ref[0, :, kb * tk:(kb + 1) * tk].T.astype(BF16)
        get_vt = lambda kb=kb: vtp_ref[0, :, kb * tk:(kb + 1) * tk].astype(BF16)
        tasks.append(_sb_task(state, 0, get_q, get_k, get_vt, lmat, None))
    _emit_pipelined(tasks, SB_SAMPLE_LAG)
    o_ref[0] = (_group_out(state["acc"][0], n, tq) * g_ref[0].astype(F32)).astype(BF16)


def _sb_sample_call(q, kn, vtn, ktp, vtp, gs, tk):
    b, t, w = q.shape
    tp = ktp.shape[2]
    new = pl.BlockSpec((1, t, SAMPLE_GROUP), lambda i, h: (i, 0, h))
    past = pl.BlockSpec((1, SAMPLE_GROUP, tp), lambda i, h: (i, h, 0))
    return pl.pallas_call(
        functools.partial(_sb_sample_kernel, tk=tk),
        grid=(b, w // SAMPLE_GROUP),
        in_specs=[new, new, pl.BlockSpec((1, 1, SAMPLE_GROUP, t), lambda i, h: (i, 0, h, 0)),
                  past, past,
                  pl.BlockSpec((t, t), lambda i, h: (0, 0)),
                  pl.BlockSpec((tk, tk), lambda i, h: (0, 0)), new],
        out_specs=new,
        out_shape=jax.ShapeDtypeStruct((b, t, w), BF16),
        compiler_params=_cparams(2),
        name="sb_sample",
    )(q, kn, vtn, ktp, vtp, _suffix_matrix(t), _suffix_matrix(tk), gs)


def _mla_task(state, c, get_q, get_k, get_vt, masked, skip_corner=False):
    t = {}

    def scores():
        t["s"] = _dot_nt(get_k(), get_q())

    def probs():
        s = t.pop("s")
        m = state["m"][c]
        if masked is None or not skip_corner:
            if masked is not None:
                s = jnp.where(masked, s, MASKED_SCORE)
            m_new = jnp.maximum(m, jnp.max(s, axis=0, keepdims=True))
            p = jnp.exp2(s - m_new).astype(BF16)
        else:
            h = s.shape[0] // 2
            top = jnp.where(masked[:h], s[:h], MASKED_SCORE)
            corner = jnp.where(masked[h:, h:], s[h:, h:], MASKED_SCORE)
            top_max = jnp.max(top, axis=0, keepdims=True)
            corner_max = jnp.max(corner, axis=0, keepdims=True)
            tile_max = jnp.concatenate(
                [top_max[:, :h], jnp.maximum(top_max[:, h:], corner_max)], axis=1)
            m_new = jnp.maximum(m, tile_max)
            p_top = jnp.exp2(top - m_new).astype(BF16)
            p_corner = jnp.exp2(corner - m_new[:, h:]).astype(BF16)
            p = jnp.concatenate(
                [p_top, jnp.concatenate([jnp.zeros((h, h), BF16), p_corner], axis=1)], axis=0)
        t["alpha"] = jnp.exp2(m - m_new)
        state["m"][c] = m_new
        t["pv"] = _dot(_with_sum_row(get_vt()), p)

    def accumulate():
        state["acc"][c] = t.pop("alpha") * state["acc"][c] + t.pop("pv")

    return [scores, probs, accumulate]


SUM_ROWS = 16


def _with_sum_row(vt):
    row = lax.broadcasted_iota(jnp.int32, (SUM_ROWS, vt.shape[1]), 0)
    return jnp.concatenate([vt, (row == 0).astype(vt.dtype)], axis=0)


def _mla_state(n, nd, nq):
    return {"m": [jnp.full((1, nq), MASKED_SCORE, F32)] * n,
            "acc": [jnp.zeros((nd + SUM_ROWS, nq), F32)] * n}


def _mla_sample_kernel(q_ref, cn_ref, krn_ref, cp_ref, krtp_ref, wknt_ref, sel_ref, wvf_ref, g_ref,
                       o_ref, *, tk):
    tq = q_ref.shape[1]
    n_heads = wknt_ref.shape[0]
    n_past = cp_ref.shape[1] // tk
    slots = [q_ref[0, :, h * LANES:(h + 1) * LANES] for h in range(n_heads)]
    qa = jnp.concatenate([_dot(s, wknt_ref[h]) for h, s in enumerate(slots)], axis=0).astype(BF16)
    qr = jnp.concatenate([_dot(s, sel_ref[...]) for s in slots], axis=0).astype(BF16)
    nq = n_heads * tq
    state = {"m": jnp.full((nq, 1), MASKED_SCORE, F32), "l": jnp.zeros((nq, 1), F32),
             "acc": jnp.zeros((nq, qa.shape[1]), F32)}

    def unit(get_latent, rope_scores):
        t = {}

        def scores():
            t["lat"] = get_latent()
            t["s"] = _dot_nt(qa, t["lat"]) + rope_scores()

        def probs():
            s = t.pop("s")
            m = state["m"]
            m_new = jnp.maximum(m, jnp.max(s, axis=1, keepdims=True))
            p = jnp.exp2(s - m_new)
            alpha = jnp.exp2(m - m_new)
            state["m"] = m_new
            state["l"] = alpha * state["l"] + jnp.sum(p, axis=1, keepdims=True)
            t["alpha"] = alpha
            t["pv"] = _dot(p.astype(BF16), t.pop("lat"))

        def accumulate():
            state["acc"] = t.pop("alpha") * state["acc"] + t.pop("pv")

        return [scores, probs, accumulate]

    tasks = [unit(lambda: cn_ref[0].astype(BF16), lambda: _dot_nt(qr, krn_ref[0]))]
    pad = jnp.zeros((LANES - MLA_ROPE_DIM, tk), BF16)
    for kb in range(n_past):
        get_latent = lambda kb=kb: cp_ref[0, kb * tk:(kb + 1) * tk, :].astype(BF16)
        rope_scores = lambda kb=kb: _dot(qr, jnp.concatenate(
            [krtp_ref[0, :, kb * tk:(kb + 1) * tk].astype(BF16), pad], axis=0))
        tasks.append(unit(get_latent, rope_scores))
    _emit_pipelined(tasks, MLA_SAMPLE_LAG)
    lat_out = (state["acc"] * (1.0 / state["l"])).astype(BF16)
    out = _dot(lat_out[0:tq], wvf_ref[0])
    for h in range(1, n_heads):
        out = out + _dot(lat_out[h * tq:(h + 1) * tq], wvf_ref[h])
    o_ref[0] = (out * g_ref[0].astype(F32)).astype(BF16)


def _mla_sample_call(q, ckv_new, kr_new, ckv_past, krt_past, gs, pw, tk, past_len):
    b, t, w = q.shape
    tp, r_kv = ckv_past.shape[1], ckv_past.shape[2]
    wo = pw["w_vf"].shape[2]
    assert gs.shape[2] == 2 * wo
    assert past_len % CHUNK == 0 and t <= CHUNK
    full = lambda a: pl.BlockSpec(a.shape, lambda i: (0,) * a.ndim)
    return pl.pallas_call(
        functools.partial(_mla_sample_kernel, tk=tk),
        grid=(b,),
        in_specs=[pl.BlockSpec((1, t, w), lambda i: (i, 0, 0)),
                  pl.BlockSpec((1, t, r_kv), lambda i: (i, 0, 0)),
                  pl.BlockSpec((1, t, LANES), lambda i: (i, 0, 0)),
                  pl.BlockSpec((1, tp, r_kv), lambda i: (i, 0, 0)),
                  pl.BlockSpec((1, MLA_ROPE_DIM, tp), lambda i: (i, 0, 0)),
                  full(pw["w_knt"]), full(pw["rope_sel"]), full(pw["w_vf"]),
                  pl.BlockSpec((1, t, wo), lambda i: (i, 0, 1))],
        out_specs=pl.BlockSpec((1, t, wo), lambda i: (i, 0, 0)),
        out_shape=jax.ShapeDtypeStruct((b, t, wo), BF16),
        compiler_params=_cparams(1),
        name="mla_sample",
    )(q, ckv_new, kr_new, ckv_past, krt_past, pw["w_knt"], pw["rope_sel"], pw["w_vf"], gs)


def _out_kernel(sbo_ref, mlao_ref, x_ref, mod_ref, w_ref, g_ref, y_ref):
    half = sbo_ref.shape[2]
    out = _dot(sbo_ref[0], w_ref[:half, :]) + _dot(mlao_ref[0], w_ref[half:, :])
    y_ref[0] = x_ref[0] + mod_ref[0, 2:3, :] * _rms(out, g_ref[...])


def _out_call(sbo, mlao, x, mod, w_out16, post_g, tm):
    b, t, d = x.shape
    row = lambda n: pl.BlockSpec((1, tm, n), lambda i, j: (i, j, 0))
    full = lambda a: pl.BlockSpec(a.shape, lambda i, j: (0,) * a.ndim)
    return pl.pallas_call(
        _out_kernel,
        grid=(b, t // tm),
        in_specs=[row(512), row(512), row(d),
                  pl.BlockSpec((1, 3, d), lambda i, j: (i, 0, 0)), full(w_out16), full(post_g)],
        out_specs=row(d),
        out_shape=jax.ShapeDtypeStruct((b, t, d), F32),
        compiler_params=_cparams(2),
        name="out_proj",
    )(sbo, mlao, x, mod, w_out16, post_g)


def _prep_weights(pre_g, w_in, q_g, w_uq, kv_g, w_ukv):
    d = w_in.shape[0]
    n_heads = w_ukv.shape[1] // (MLA_NOPE_DIM + MLA_V_DIM)
    half = MLA_ROPE_DIM // 2
    sizes = (512, 512, 512, 512, w_uq.shape[0], w_ukv.shape[0], MLA_ROPE_DIM, 512)
    starts = [0]
    for s in sizes:
        starts.append(starts[-1] + s)
    cols = lambda i: w_in[:, starts[i]:starts[i + 1]]
    kr = cols(6)
    kr_swapped = jnp.concatenate([-kr[:, half:], kr[:, :half]], axis=1)
    zpad = jnp.zeros((d, LANES - 2 * MLA_ROPE_DIM), w_in.dtype)
    w_aug = jnp.concatenate([cols(0), cols(1), cols(2), cols(3), cols(4), cols(5), cols(7),
                             kr, kr_swapped, zpad], axis=1).astype(BF16)
    assert w_aug.shape[1] == _C_END

    r_q = w_uq.shape[0]
    uq = w_uq.reshape(r_q, n_heads, MLA_NOPE_DIM + MLA_ROPE_DIM)
    zq = jnp.zeros((r_q, n_heads, LANES - MLA_NOPE_DIM - MLA_ROPE_DIM), w_uq.dtype)
    w_a = jnp.concatenate([uq, zq], axis=-1).reshape(r_q, n_heads * LANES).astype(BF16)

    r_kv = w_ukv.shape[0]
    ukv = w_ukv.reshape(r_kv, n_heads, MLA_NOPE_DIM + MLA_V_DIM)
    uk, uv = ukv[..., :MLA_NOPE_DIM], ukv[..., MLA_NOPE_DIM:]
    zk = jnp.zeros((r_kv, n_heads, LANES - MLA_NOPE_DIM), w_ukv.dtype)
    w_kn = jnp.concatenate([uk, zk], axis=-1).reshape(r_kv, n_heads * LANES).astype(BF16)
    w_v = uv.reshape(r_kv, n_heads * MLA_V_DIM).astype(BF16)

    w_knt = jnp.transpose(jnp.concatenate([uk, zk], axis=-1), (1, 2, 0)).astype(BF16)
    eye = jnp.eye(n_heads, dtype=w_ukv.dtype)
    w_vf = jnp.einsum("rhd,hg->hrgd", uv, eye).reshape(n_heads, r_kv, n_heads * MLA_V_DIM).astype(BF16)
    src = lax.broadcasted_iota(jnp.int32, (LANES, LANES), 0)
    dst = lax.broadcasted_iota(jnp.int32, (LANES, LANES), 1)
    rope_sel = ((src == dst + MLA_NOPE_DIM) & (dst < MLA_ROPE_DIM)).astype(BF16)

    return dict(pre_g=pre_g[None, :], w_aug=w_aug, q_g=q_g[None, :], w_a=w_a,
                kv_g=kv_g[None, :], w_kn=w_kn, w_v=w_v, w_knt=w_knt, w_vf=w_vf, rope_sel=rope_sel)


def _rope_tables(pos):
    r = MLA_ROPE_DIM
    half = r // 2
    inv_freq = ROPE_THETA ** (-jnp.arange(0, r, 2, dtype=F32) / r)
    ang = pos.astype(F32)[:, None] * inv_freq[None, :]
    cos, sin = jnp.cos(ang), jnp.sin(ang)
    t = pos.shape[0]
    scale = (MLA_NOPE_DIM + MLA_ROPE_DIM) ** -0.5 * LOG2E
    z = lambda n: jnp.zeros((t, n), F32)
    cq = jnp.concatenate([jnp.full((t, MLA_NOPE_DIM), scale, F32), cos * scale, cos * scale,
                          z(LANES - MLA_NOPE_DIM - r)], axis=1)
    s1 = jnp.concatenate([z(MLA_NOPE_DIM + half), sin * scale, z(LANES - MLA_NOPE_DIM - r)], axis=1)
    s2 = jnp.concatenate([z(MLA_NOPE_DIM), -sin * scale, z(LANES - MLA_NOPE_DIM - half)], axis=1)
    ck = jnp.concatenate([cos, cos, z(LANES - r)], axis=1)
    sk = jnp.concatenate([sin, sin, z(LANES - r)], axis=1)
    return dict(cq=cq, s1=s1, s2=s2, ck=ck, sk=sk)


PROMPT_T = 256
PROJ_TM = 512
OUT_TM = 512
PAST_TK = 256
SAMPLE_GROUP = 256
SB_LAG = 2
MLA_LAG = 6
SB_SAMPLE_LAG = 4
MLA_SAMPLE_LAG = 3


def kernel(x_prompt, x_sample, cache_sb_k, cache_sb_v, cache_mla_ckv, cache_mla_krope, c_prompt, c_sample, ada_w, ada_b, pre_norm_g, w_in, q_norm_g, w_uq, kv_norm_g, w_ukv, w_out, post_norm_g):
    depth = ada_w.shape[0]
    bp, tp, d = x_prompt.shape
    bs, ts, _ = x_sample.shape
    past_len = cache_sb_k.shape[2]
    n_sb = cache_sb_k.shape[3]

    tabs_p = _rope_tables(jnp.arange(tp, dtype=jnp.int32))
    tabs_s = _rope_tables(past_len + jnp.arange(ts, dtype=jnp.int32))
    c_all = jnp.concatenate([c_prompt, c_sample], axis=0)

    yp, ys = x_prompt, x_sample
    new_p, new_s = [], []
    for l in range(depth):
        pw = _prep_weights(pre_norm_g[l], w_in[l], q_norm_g[l], w_uq[l], kv_norm_g[l], w_ukv[l])
        w_out16 = w_out[l].astype(BF16)
        post_g = post_norm_g[l][None, :]
        mod = _mod_call(c_all, ada_w[l].astype(BF16), ada_b[l][None, :])
        mod_p = mod[:bp].reshape(bp, 3, d)
        mod_s = mod[bp:].reshape(bs, 3, d)

        (sbq, sbk32, sbk16, sbv32, sbvt, gs, qcat, ckv, kcat, mvt, kr, _) = _proj_call(
            yp, mod_p, pw, tabs_p, PROJ_TM, PROMPT_T)
        sbo = _sb_prompt_call(sbq, sbk16, sbvt, gs, PROMPT_T)
        mlao = _mla_prompt_call(qcat, kcat, mvt, gs, PROMPT_T)
        yp = _out_call(sbo, mlao, yp, mod_p, w_out16, post_g, OUT_TM)
        new_p.append((sbk32.reshape(bp, tp, n_sb, SB_HEAD_DIM), sbv32.reshape(bp, tp, n_sb, SB_HEAD_DIM),
                      ckv, kr))

        (sbq, sbk32, sbk16, sbv32, sbvt, gs, qcat, ckv, _, _, kr, krp) = _proj_call(
            ys, mod_s, pw, tabs_s, ts, ts)
        ktp = jnp.transpose(cache_sb_k[l], (0, 2, 3, 1)).reshape(bs, n_sb * SB_HEAD_DIM, past_len)
        vtp = jnp.transpose(cache_sb_v[l], (0, 2, 3, 1)).reshape(bs, n_sb * SB_HEAD_DIM, past_len)
        sbo = _sb_sample_call(sbq, sbk16, sbvt, ktp, vtp, gs, PAST_TK)
        krt_p = jnp.transpose(cache_mla_krope[l], (0, 2, 1))
        mlao = _mla_sample_call(qcat, ckv, krp, cache_mla_ckv[l], krt_p, gs, pw, PAST_TK, past_len)
        ys = _out_call(sbo, mlao, ys, mod_s, w_out16, post_g, ts)
        new_s.append((sbk32.reshape(bs, ts, n_sb, SB_HEAD_DIM), sbv32.reshape(bs, ts, n_sb, SB_HEAD_DIM),
                      ckv, kr))

    stack = lambda items, i: items[0][i][None] if depth == 1 else jnp.stack([it[i] for it in items])
    return (yp, ys,
            stack(new_p, 0), stack(new_p, 1), stack(new_p, 2), stack(new_p, 3),
            stack(new_s, 0), stack(new_s, 1), stack(new_s, 2), stack(new_s, 3))
```

```python
import functools

import jax
import jax.numpy as jnp
from jax import lax
from jax.experimental import pallas as pl
from jax.experimental.pallas import tpu as pltpu

F32 = jnp.float32
BF16 = jnp.bfloat16

CHUNK = 64
SB_HEAD_DIM = 64
MLA_V_DIM = 64
MLA_NOPE_DIM = 64
MLA_ROPE_DIM = 32
ROPE_THETA = 10000.0
EPS = 1e-6
LOG2E = 1.4426950408889634
LANES = 128
HEAD = 64
MASKED_SCORE = -1e30

VMEM_LIMIT = 56 * 1024 * 1024


def _cparams(n_grid):
    return pltpu.CompilerParams(dimension_semantics=("arbitrary",) * n_grid,
                                vmem_limit_bytes=VMEM_LIMIT)


def _silu(x):
    return x * (1.0 / (1.0 + jnp.exp(-x)))


def _rms(x, g):
    return x * lax.rsqrt(jnp.mean(x * x, axis=-1, keepdims=True) + EPS) * g


def _dot(a, b):
    return jnp.dot(a, b, preferred_element_type=F32)


def _dot_nt(a, b):
    return lax.dot_general(a, b, (((1,), (1,)), ((), ())), preferred_element_type=F32)


def _mod_kernel(c_ref, w_ref, b_ref, o_ref):
    s = _silu(c_ref[...]).astype(BF16)
    o_ref[...] = _dot(s, w_ref[...]) + b_ref[...]


def _mod_call(c_all, ada_w16, ada_b):
    n, d3 = c_all.shape[0], ada_w16.shape[1]
    return pl.pallas_call(
        _mod_kernel,
        out_shape=jax.ShapeDtypeStruct((n, d3), F32),
        compiler_params=pltpu.CompilerParams(vmem_limit_bytes=VMEM_LIMIT),
        name="ada_mod",
    )(c_all, ada_w16, ada_b)


_C_SBQ, _C_SBK, _C_SBV, _C_SBG = 0, 512, 1024, 1536
_C_CQ, _C_CKV, _C_MLAG, _C_KR, _C_END = 2048, 2432, 2688, 3200, 3328
_HALF_ROPE = MLA_ROPE_DIM // 2


def _store_key_blocks_transposed(ref, v):
    tb = ref.shape[3]
    for i in range(ref.shape[1]):
        ref[0, i] = v[i * tb:(i + 1) * tb].T.astype(BF16)


def _proj_kernel(x_ref, mod_ref, pg_ref, w_ref, qg_ref, wa_ref, kvg_ref, wkn_ref, wv_ref,
                 cq_ref, s1_ref, s2_ref, ck_ref, sk_ref,
                 sbq_ref, sbk32_ref, sbk16_ref, sbv32_ref, sbvt_ref, gs_ref, qcat_ref,
                 ckv_ref, kcat_ref, mvt_ref, kr_ref, krp_ref):
    x = x_ref[0]
    rs = lax.rsqrt(jnp.mean(x * x, axis=-1, keepdims=True) + EPS)
    g = pg_ref[...] * (1.0 + mod_ref[0, 1:2, :])
    hb = ((x * rs) * g + mod_ref[0, 0:1, :]).astype(BF16)

    def proj(a, b):
        return _dot(hb, w_ref[:, a:b])

    cq = proj(_C_CQ, _C_CKV)
    ckv = proj(_C_CKV, _C_MLAG)
    r = proj(_C_KR, _C_END)

    v = proj(_C_SBV, _C_SBG)
    sbv32_ref[0] = v
    _store_key_blocks_transposed(sbvt_ref, v)
    gs_ref[0, :, 0:512] = _silu(proj(_C_SBG, _C_CQ)).astype(BF16)
    gs_ref[0, :, 512:1024] = _silu(proj(_C_MLAG, _C_KR)).astype(BF16)

    qa = _dot(_rms(cq, qg_ref[...]).astype(BF16), wa_ref[...])
    cq_t, s1_t, s2_t = cq_ref[...], s1_ref[...], s2_ref[...]
    n_heads = qcat_ref.shape[2] // LANES
    for h in range(n_heads):
        a = qa[:, h * LANES:(h + 1) * LANES]
        up = pltpu.roll(a, _HALF_ROPE, 1)
        down = pltpu.roll(a, LANES - _HALF_ROPE, 1)
        qcat_ref[0, :, h * LANES:(h + 1) * LANES] = (a * cq_t + up * s1_t + down * s2_t).astype(BF16)

    ckvn = _rms(ckv, kvg_ref[...])
    ckv_ref[0] = ckvn
    cb = ckvn.astype(BF16)
    _store_key_blocks_transposed(mvt_ref, _dot(cb, wv_ref[...]))
    kro = r * ck_ref[...] + pltpu.roll(r, LANES - MLA_ROPE_DIM, 1) * sk_ref[...]
    kr_ref[0] = kro[:, :MLA_ROPE_DIM]
    krp_ref[0] = kro.astype(BF16)
    slot = pltpu.roll(kro, MLA_NOPE_DIM, 1)
    kn = _dot(cb, wkn_ref[...])
    for h in range(n_heads):
        kcat_ref[0, :, h * LANES:(h + 1) * LANES] = (kn[:, h * LANES:(h + 1) * LANES] + slot).astype(BF16)

    k = proj(_C_SBK, _C_SBV)
    sbk32_ref[0] = k
    sbk16_ref[0] = k.astype(BF16)
    sbq_ref[0] = (proj(_C_SBQ, _C_SBK) * (SB_HEAD_DIM ** -0.5 * LOG2E)).astype(BF16)


def _proj_call(x, mod, pw, tabs, tm, tb):
    b, t, d = x.shape
    nt = t // tm
    row = lambda n: pl.BlockSpec((1, tm, n), lambda i, j: (i, j, 0))
    colt = pl.BlockSpec((1, tm // tb, 512, tb), lambda i, j: (i, j, 0, 0))
    full = lambda a: pl.BlockSpec(a.shape, lambda i, j: (0,) * a.ndim)
    tab = pl.BlockSpec((tm, LANES), lambda i, j: (j, 0))
    o = lambda n, dt: jax.ShapeDtypeStruct((b, t, n), dt)
    ot = jax.ShapeDtypeStruct((b, t // tb, 512, tb), BF16)
    return pl.pallas_call(
        _proj_kernel,
        grid=(b, nt),
        in_specs=[row(d), pl.BlockSpec((1, 3, d), lambda i, j: (i, 0, 0)), full(pw["pre_g"]),
                  full(pw["w_aug"]), full(pw["q_g"]), full(pw["w_a"]), full(pw["kv_g"]),
                  full(pw["w_kn"]), full(pw["w_v"]), tab, tab, tab, tab, tab],
        out_specs=[row(512), row(512), row(512), row(512), colt, row(1024), row(1024),
                   row(256), row(1024), colt, row(MLA_ROPE_DIM), row(LANES)],
        out_shape=[o(512, BF16), o(512, F32), o(512, BF16), o(512, F32), ot,
                   o(1024, BF16), o(1024, BF16), o(256, F32), o(1024, BF16), ot,
                   o(MLA_ROPE_DIM, F32), o(LANES, BF16)],
        compiler_params=_cparams(2),
        name="in_proj",
    )(x, mod, pw["pre_g"], pw["w_aug"], pw["q_g"], pw["w_a"], pw["kv_g"], pw["w_kn"], pw["w_v"],
      tabs["cq"], tabs["s1"], tabs["s2"], tabs["ck"], tabs["sk"])


def _emit_pipelined(tasks, lag):
    n_slots = max(j + (len(stages) - 1) * lag for j, stages in enumerate(tasks)) + 1
    for slot in range(n_slots):
        for j, stages in enumerate(tasks):
            s, rem = divmod(slot - j, lag)
            if slot >= j and rem == 0 and s < len(stages):
                stages[s]()


def _sb_task(state, c, get_q, get_k, get_vt, lmat, masked, skip_corner=False):
    t = {}

    def log_rem(z):
        low = jnp.minimum(z, 0.0)
        neg_relu = low - z
        l = jnp.log2(1.0 + jnp.exp2(low + neg_relu))
        return (neg_relu - l).astype(BF16)

    def scores():
        t["z"] = _dot_nt(get_k(), get_q())

    def logs():
        z = t.pop("z")
        if masked is None:
            t["z"] = [z]
            lr = log_rem(z)
        elif not skip_corner:
            z = jnp.where(masked, z, MASKED_SCORE)
            t["z"] = [z]
            lr = log_rem(z)
        else:
            h = z.shape[0] // 2
            top = jnp.where(masked[:h], z[:h], MASKED_SCORE)
            corner = jnp.where(masked[h:, h:], z[h:, h:], MASKED_SCORE)
            t["z"] = [top, corner]
            lr = jnp.concatenate(
                [log_rem(top), jnp.concatenate([jnp.zeros((h, h), BF16), log_rem(corner)], axis=1)],
                axis=0)
        t["tail"] = _dot(lmat, lr)

    def weights():
        tail = t.pop("tail")
        zs = t.pop("z")
        if len(zs) == 1:
            w = jnp.exp2(zs[0] + tail).astype(BF16)
        else:
            h = zs[0].shape[0]
            w_top = jnp.exp2(zs[0] + tail[:h]).astype(BF16)
            w_corner = jnp.exp2(zs[1] + tail[h:, h:]).astype(BF16)
            w = jnp.concatenate(
                [w_top, jnp.concatenate([jnp.zeros((h, h), BF16), w_corner], axis=1)], axis=0)
        r = state["r"][c]
        t["scale"] = jnp.exp2(r)
        state["r"][c] = r + tail[0:1, :]
        t["pv"] = _dot(get_vt(), w)

    def accumulate():
        state["acc"][c] = state["acc"][c] + t.pop("pv") * t.pop("scale")

    return [scores, logs, weights, accumulate]


def _causal_units(nt):
    return [(i, i - dist) for dist in range(nt) for i in range(dist, nt)]


def _store_gated(o_ref, g_ref, rows, dims_by_query):
    o_ref[0, rows, :] = (dims_by_query.T * g_ref[0, rows, :].astype(F32)).astype(BF16)


def _sb_prompt_kernel(q_ref, k_ref, vt_ref, l_ref, g_ref, o_ref, *, tb):
    nt = q_ref.shape[1] // tb
    key = lax.broadcasted_iota(jnp.int32, (tb, tb), 0)
    qry = lax.broadcasted_iota(jnp.int32, (tb, tb), 1)
    causal = key < qry
    lmat = l_ref[...]
    state = {"r": [jnp.zeros((1, tb), F32)] * (2 * nt), "acc": [jnp.zeros((HEAD, tb), F32)] * (2 * nt)}
    qs = {}

    def unit(i, j, h):
        def get_q():
            if (i, h) not in qs:
                rows = _group_rows(q_ref[0, i * tb:(i + 1) * tb, :], 2)
                qs[(i, 0)], qs[(i, 1)] = rows[:tb], rows[tb:]
            return qs[(i, h)]

        def write_tile():
            pair = jnp.concatenate([state["acc"][2 * i], state["acc"][2 * i + 1]], axis=0)
            _store_gated(o_ref, g_ref, slice(i * tb, (i + 1) * tb), pair)

        get_k = lambda: k_ref[0, j * tb:(j + 1) * tb, :]
        get_vt = lambda: vt_ref[0, j, h * HEAD:(h + 1) * HEAD, :]
        stages = _sb_task(state, 2 * i + h, get_q, get_k, get_vt, lmat,
                          causal if i == j else None, skip_corner=True)
        return stages + [write_tile] if (j == 0 and h == 1) else stages

    _emit_pipelined([unit(i, j, h) for i, j in _causal_units(nt) for h in range(2)], SB_LAG)


def _mla_prompt_kernel(q_ref, k_ref, vt_ref, g_ref, o_ref, *, tb):
    nt = q_ref.shape[1] // tb
    key = lax.broadcasted_iota(jnp.int32, (tb, tb), 0)
    qry = lax.broadcasted_iota(jnp.int32, (tb, tb), 1)
    visible = (key // CHUNK) <= (qry // CHUNK)
    state = _mla_state(2 * nt, HEAD, tb)

    def unit(i, j, h):
        def write_tile():
            accs = [state["acc"][2 * i + hh] for hh in range(2)]
            outs = [a[:HEAD] * (1.0 / a[HEAD:HEAD + 1]) for a in accs]
            _store_gated(o_ref, g_ref, slice(i * tb, (i + 1) * tb), jnp.concatenate(outs, axis=0))

        get_q = lambda: q_ref[0, i * tb:(i + 1) * tb, h * LANES:(h + 1) * LANES]
        get_k = lambda: k_ref[0, j * tb:(j + 1) * tb, h * LANES:(h + 1) * LANES]
        get_vt = lambda: vt_ref[0, j, h * HEAD:(h + 1) * HEAD, :]
        stages = _mla_task(state, 2 * i + h, get_q, get_k, get_vt,
                           visible if i == j else None, skip_corner=True)
        return stages + [write_tile] if (j == 0 and h == 1) else stages

    _emit_pipelined([unit(i, j, h) for i, j in _causal_units(nt) for h in range(2)], MLA_LAG)


def _suffix_matrix(n):
    s = lax.broadcasted_iota(jnp.int32, (n, n), 0)
    j = lax.broadcasted_iota(jnp.int32, (n, n), 1)
    return (j >= s).astype(BF16)


def _sb_prompt_call(q, k, vt, gs, tb):
    b, t, w = q.shape
    pair = pl.BlockSpec((1, t, LANES), lambda i, h: (i, 0, h))
    return pl.pallas_call(
        functools.partial(_sb_prompt_kernel, tb=tb),
        grid=(b, w // LANES),
        in_specs=[pair, pair, pl.BlockSpec((1, t // tb, LANES, tb), lambda i, h: (i, 0, h, 0)),
                  pl.BlockSpec((tb, tb), lambda i, h: (0, 0)), pair],
        out_specs=pair,
        out_shape=jax.ShapeDtypeStruct((b, t, w), BF16),
        compiler_params=_cparams(2),
        name="sb_prompt",
    )(q, k, vt, _suffix_matrix(tb), gs)


def _mla_prompt_call(q, k, vt, gs, tb):
    b, t, w2 = q.shape
    w = w2 // 2
    n_pairs = w // LANES
    slots = pl.BlockSpec((1, t, 2 * LANES), lambda i, h: (i, 0, h))
    return pl.pallas_call(
        functools.partial(_mla_prompt_kernel, tb=tb),
        grid=(b, n_pairs),
        in_specs=[slots, slots, pl.BlockSpec((1, t // tb, LANES, tb), lambda i, h: (i, 0, h, 0)),
                  pl.BlockSpec((1, t, LANES), lambda i, h: (i, 0, n_pairs + h))],
        out_specs=pl.BlockSpec((1, t, LANES), lambda i, h: (i, 0, h)),
        out_shape=jax.ShapeDtypeStruct((b, t, w), BF16),
        compiler_params=_cparams(2),
        name="mla_prompt",
    )(q, k, vt, gs)


def _group_rows(qg, n):
    lane = lax.broadcasted_iota(jnp.int32, qg.shape, 1)
    qf = qg.astype(F32)
    return jnp.concatenate(
        [jnp.where(lane // HEAD == h, qf, 0.0).astype(BF16) for h in range(n)], axis=0)


def _group_out(acc, n, t):
    acc_t = acc.T
    lane = lax.broadcasted_iota(jnp.int32, (t, acc.shape[0]), 1)
    out = acc_t[:t]
    for h in range(1, n):
        out = jnp.where(lane // HEAD == h, acc_t[h * t:(h + 1) * t], out)
    return out


def _sb_sample_kernel(q_ref, kn_ref, vtn_ref, ktp_ref, vtp_ref, ls_ref, l_ref, g_ref, o_ref, *, tk):
    tq, width = q_ref.shape[1], q_ref.shape[2]
    n = width // HEAD
    n_past = ktp_ref.shape[2] // tk
    qrows = _group_rows(q_ref[0], n)
    key = lax.broadcasted_iota(jnp.int32, (tq, n * tq), 0)
    qry = lax.broadcasted_iota(jnp.int32, (tq, n * tq), 1) % tq
    causal = key < qry
    lmat = l_ref[...]
    state = {"r": [jnp.zeros((1, n * tq), F32)], "acc": [jnp.zeros((width, n * tq), F32)]}
    get_q = lambda: qrows
    tasks = [_sb_task(state, 0, get_q, lambda: kn_ref[0], lambda: vtn_ref[0, 0], ls_ref[...], causal)]
    for kb in reversed(range(n_past)):
        get_k = lambda kb=kb: ktp_ref[0, :, kb * tk:(kb + 1) * tk].T.astype(BF16)
        get_vt = lambda kb=kb: vtp_ref[0, :, kb * tk:(kb + 1) * tk].astype(BF16)
        tasks.append(_sb_task(state, 0, get_q, get_k, get_vt, lmat, None))
    _emit_pipelined(tasks, SB_SAMPLE_LAG)
    o_ref[0] = (_group_out(state["acc"][0], n, tq) * g_ref[0].astype(F32)).astype(BF16)


def _sb_sample_call(q, kn, vtn, ktp, vtp, gs, tk):
    b, t, w = q.shape
    tp = ktp.shape[2]
    new = pl.BlockSpec((1, t, SAMPLE_GROUP), lambda i, h: (i, 0, h))
    past = pl.BlockSpec((1, SAMPLE_GROUP, tp), lambda i, h: (i, h, 0))
    return pl.pallas_call(
        functools.partial(_sb_sample_kernel, tk=tk),
        grid=(b, w // SAMPLE_GROUP),
        in_specs=[new, new, pl.BlockSpec((1, 1, SAMPLE_GROUP, t), lambda i, h: (i, 0, h, 0)),
                  past, past,
                  pl.BlockSpec((t, t), lambda i, h: (0, 0)),
                  pl.BlockSpec((tk, tk), lambda i, h: (0, 0)), new],
        out_specs=new,
        out_shape=jax.ShapeDtypeStruct((b, t, w), BF16),
        compiler_params=_cparams(2),
        name="sb_sample",
    )(q, kn, vtn, ktp, vtp, _suffix_matrix(t), _suffix_matrix(tk), gs)


def _mla_task(state, c, get_q, get_k, get_vt, masked, skip_corner=False):
    t = {}

    def scores():
        t["s"] = _dot_nt(get_k(), get_q())

    def probs():
        s = t.pop("s")
        m = state["m"][c]
        if masked is None or not skip_corner:
            if masked is not None:
                s = jnp.where(masked, s, MASKED_SCORE)
            m_new = jnp.maximum(m, jnp.max(s, axis=0, keepdims=True))
            p = jnp.exp2(s - m_new).astype(BF16)
        else:
            h = s.shape[0] // 2
            top = jnp.where(masked[:h], s[:h], MASKED_SCORE)
            corner = jnp.where(masked[h:, h:], s[h:, h:], MASKED_SCORE)
            top_max = jnp.max(top, axis=0, keepdims=True)
            corner_max = jnp.max(corner, axis=0, keepdims=True)
            tile_max = jnp.concatenate(
                [top_max[:, :h], jnp.maximum(top_max[:, h:], corner_max)], axis=1)
            m_new = jnp.maximum(m, tile_max)
            p_top = jnp.exp2(top - m_new).astype(BF16)
            p_corner = jnp.exp2(corner - m_new[:, h:]).astype(BF16)
            p = jnp.concatenate(
                [p_top, jnp.concatenate([jnp.zeros((h, h), BF16), p_corner], axis=1)], axis=0)
        t["alpha"] = jnp.exp2(m - m_new)
        state["m"][c] = m_new
        t["pv"] = _dot(_with_sum_row(get_vt()), p)

    def accumulate():
        state["acc"][c] = t.pop("alpha") * state["acc"][c] + t.pop("pv")

    return [scores, probs, accumulate]


SUM_ROWS = 16


def _with_sum_row(vt):
    row = lax.broadcasted_iota(jnp.int32, (SUM_ROWS, vt.shape[1]), 0)
    return jnp.concatenate([vt, (row == 0).astype(vt.dtype)], axis=0)


def _mla_state(n, nd, nq):
    return {"m": [jnp.full((1, nq), MASKED_SCORE, F32)] * n,
            "acc": [jnp.zeros((nd + SUM_ROWS, nq), F32)] * n}


def _mla_sample_kernel(q_ref, cn_ref, krn_ref, cp_ref, krtp_ref, wknt_ref, sel_ref, wvf_ref, g_ref,
                       o_ref, *, tk):
    tq = q_ref.shape[1]
    n_heads = wknt_ref.shape[0]
    n_past = cp_ref.shape[1] // tk
    slots = [q_ref[0, :, h * LANES:(h + 1) * LANES] for h in range(n_heads)]
    qa = jnp.concatenate([_dot(s, wknt_ref[h]) for h, s in enumerate(slots)], axis=0).astype(BF16)
    qr = jnp.concatenate([_dot(s, sel_ref[...]) for s in slots], axis=0).astype(BF16)
    nq = n_heads * tq
    state = {"m": jnp.full((nq, 1), MASKED_SCORE, F32), "l": jnp.zeros((nq, 1), F32),
             "acc": jnp.zeros((nq, qa.shape[1]), F32)}

    def unit(get_latent, rope_scores):
        t = {}

        def scores():
            t["lat"] = get_latent()
            t["s"] = _dot_nt(qa, t["lat"]) + rope_scores()

        def probs():
            s = t.pop("s")
            m = state["m"]
            m_new = jnp.maximum(m, jnp.max(s, axis=1, keepdims=True))
            p = jnp.exp2(s - m_new)
            alpha = jnp.exp2(m - m_new)
            state["m"] = m_new
            state["l"] = alpha * state["l"] + jnp.sum(p, axis=1, keepdims=True)
            t["alpha"] = alpha
            t["pv"] = _dot(p.astype(BF16), t.pop("lat"))

        def accumulate():
            state["acc"] = t.pop("alpha") * state["acc"] + t.pop("pv")

        return [scores, probs, accumulate]

    tasks = [unit(lambda: cn_ref[0].astype(BF16), lambda: _dot_nt(qr, krn_ref[0]))]
    pad = jnp.zeros((LANES - MLA_ROPE_DIM, tk), BF16)
    for kb in range(n_past):
        get_latent = lambda kb=kb: cp_ref[0, kb * tk:(kb + 1) * tk, :].astype(BF16)
        rope_scores = lambda kb=kb: _dot(qr, jnp.concatenate(
            [krtp_ref[0, :, kb * tk:(kb + 1) * tk].astype(BF16), pad], axis=0))
        tasks.append(unit(get_latent, rope_scores))
    _emit_pipelined(tasks, MLA_SAMPLE_LAG)
    lat_out = (state["acc"] * (1.0 / state["l"])).astype(BF16)
    out = _dot(lat_out[0:tq], wvf_ref[0])
    for h in range(1, n_heads):
        out = out + _dot(lat_out[h * tq:(h + 1) * tq], wvf_ref[h])
    o_ref[0] = (out * g_ref[0].astype(F32)).astype(BF16)


def _mla_sample_call(q, ckv_new, kr_new, ckv_past, krt_past, gs, pw, tk, past_len):
    b, t, w = q.shape
    tp, r_kv = ckv_past.shape[1], ckv_past.shape[2]
    wo = pw["w_vf"].shape[2]
    assert gs.shape[2] == 2 * wo
    assert past_len % CHUNK == 0 and t <= CHUNK
    full = lambda a: pl.BlockSpec(a.shape, lambda i: (0,) * a.ndim)
    return pl.pallas_call(
        functools.partial(_mla_sample_kernel, tk=tk),
        grid=(b,),
        in_specs=[pl.BlockSpec((1, t, w), lambda i: (i, 0, 0)),
                  pl.BlockSpec((1, t, r_kv), lambda i: (i, 0, 0)),
                  pl.BlockSpec((1, t, LANES), lambda i: (i, 0, 0)),
                  pl.BlockSpec((1, tp, r_kv), lambda i: (i, 0, 0)),
                  pl.BlockSpec((1, MLA_ROPE_DIM, tp), lambda i: (i, 0, 0)),
                  full(pw["w_knt"]), full(pw["rope_sel"]), full(pw["w_vf"]),
                  pl.BlockSpec((1, t, wo), lambda i: (i, 0, 1))],
        out_specs=pl.BlockSpec((1, t, wo), lambda i: (i, 0, 0)),
        out_shape=jax.ShapeDtypeStruct((b, t, wo), BF16),
        compiler_params=_cparams(1),
        name="mla_sample",
    )(q, ckv_new, kr_new, ckv_past, krt_past, pw["w_knt"], pw["rope_sel"], pw["w_vf"], gs)


def _out_kernel(sbo_ref, mlao_ref, x_ref, mod_ref, w_ref, g_ref, y_ref):
    half = sbo_ref.shape[2]
    out = _dot(sbo_ref[0], w_ref[:half, :]) + _dot(mlao_ref[0], w_ref[half:, :])
    y_ref[0] = x_ref[0] + mod_ref[0, 2:3, :] * _rms(out, g_ref[...])


def _out_call(sbo, mlao, x, mod, w_out16, post_g, tm):
    b, t, d = x.shape
    row = lambda n: pl.BlockSpec((1, tm, n), lambda i, j: (i, j, 0))
    full = lambda a: pl.BlockSpec(a.shape, lambda i, j: (0,) * a.ndim)
    return pl.pallas_call(
        _out_kernel,
        grid=(b, t // tm),
        in_specs=[row(512), row(512), row(d),
                  pl.BlockSpec((1, 3, d), lambda i, j: (i, 0, 0)), full(w_out16), full(post_g)],
        out_specs=row(d),
        out_shape=jax.ShapeDtypeStruct((b, t, d), F32),
        compiler_params=_cparams(2),
        name="out_proj",
    )(sbo, mlao, x, mod, w_out16, post_g)


def _prep_weights(pre_g, w_in, q_g, w_uq, kv_g, w_ukv):
    d = w_in.shape[0]
    n_heads = w_ukv.shape[1] // (MLA_NOPE_DIM + MLA_V_DIM)
    half = MLA_ROPE_DIM // 2
    sizes = (512, 512, 512, 512, w_uq.shape[0], w_ukv.shape[0], MLA_ROPE_DIM, 512)
    starts = [0]
    for s in sizes:
        starts.append(starts[-1] + s)
    cols = lambda i: w_in[:, starts[i]:starts[i + 1]]
    kr = cols(6)
    kr_swapped = jnp.concatenate([-kr[:, half:], kr[:, :half]], axis=1)
    zpad = jnp.zeros((d, LANES - 2 * MLA_ROPE_DIM), w_in.dtype)
    w_aug = jnp.concatenate([cols(0), cols(1), cols(2), cols(3), cols(4), cols(5), cols(7),
                             kr, kr_swapped, zpad], axis=1).astype(BF16)
    assert w_aug.shape[1] == _C_END

    r_q = w_uq.shape[0]
    uq = w_uq.reshape(r_q, n_heads, MLA_NOPE_DIM + MLA_ROPE_DIM)
    zq = jnp.zeros((r_q, n_heads, LANES - MLA_NOPE_DIM - MLA_ROPE_DIM), w_uq.dtype)
    w_a = jnp.concatenate([uq, zq], axis=-1).reshape(r_q, n_heads * LANES).astype(BF16)

    r_kv = w_ukv.shape[0]
    ukv = w_ukv.reshape(r_kv, n_heads, MLA_NOPE_DIM + MLA_V_DIM)
    uk, uv = ukv[..., :MLA_NOPE_DIM], ukv[..., MLA_NOPE_DIM:]
    zk = jnp.zeros((r_kv, n_heads, LANES - MLA_NOPE_DIM), w_ukv.dtype)
    w_kn = jnp.concatenate([uk, zk], axis=-1).reshape(r_kv, n_heads * LANES).astype(BF16)
    w_v = uv.reshape(r_kv, n_heads * MLA_V_DIM).astype(BF16)

    w_knt = jnp.transpose(jnp.concatenate([uk, zk], axis=-1), (1, 2, 0)).astype(BF16)
    eye = jnp.eye(n_heads, dtype=w_ukv.dtype)
    w_vf = jnp.einsum("rhd,hg->hrgd", uv, eye).reshape(n_heads, r_kv, n_heads * MLA_V_DIM).astype(BF16)
    src = lax.broadcasted_iota(jnp.int32, (LANES, LANES), 0)
    dst = lax.broadcasted_iota(jnp.int32, (LANES, LANES), 1)
    rope_sel = ((src == dst + MLA_NOPE_DIM) & (dst < MLA_ROPE_DIM)).astype(BF16)

    return dict(pre_g=pre_g[None, :], w_aug=w_aug, q_g=q_g[None, :], w_a=w_a,
                kv_g=kv_g[None, :], w_kn=w_kn, w_v=w_v, w_knt=w_knt, w_vf=w_vf, rope_sel=rope_sel)


def _rope_tables(pos):
    r = MLA_ROPE_DIM
    half = r // 2
    inv_freq = ROPE_THETA ** (-jnp.arange(0, r, 2, dtype=F32) / r)
    ang = pos.astype(F32)[:, None] * inv_freq[None, :]
    cos, sin = jnp.cos(ang), jnp.sin(ang)
    t = pos.shape[0]
    scale = (MLA_NOPE_DIM + MLA_ROPE_DIM) ** -0.5 * LOG2E
    z = lambda n: jnp.zeros((t, n), F32)
    cq = jnp.concatenate([jnp.full((t, MLA_NOPE_DIM), scale, F32), cos * scale, cos * scale,
                          z(LANES - MLA_NOPE_DIM - r)], axis=1)
    s1 = jnp.concatenate([z(MLA_NOPE_DIM + half), sin * scale, z(LANES - MLA_NOPE_DIM - r)], axis=1)
    s2 = jnp.concatenate([z(MLA_NOPE_DIM), -sin * scale, z(LANES - MLA_NOPE_DIM - half)], axis=1)
    ck = jnp.concatenate([cos, cos, z(LANES - r)], axis=1)
    sk = jnp.concatenate([sin, sin, z(LANES - r)], axis=1)
    return dict(cq=cq, s1=s1, s2=s2, ck=ck, sk=sk)


PROMPT_T = 256
PROJ_TM = 512
OUT_TM = 1024
PAST_TK = 256
SAMPLE_GROUP = 256
SB_LAG = 2
MLA_LAG = 6
SB_SAMPLE_LAG = 4
MLA_SAMPLE_LAG = 3


def kernel(x_prompt, x_sample, cache_sb_k, cache_sb_v, cache_mla_ckv, cache_mla_krope, c_prompt, c_sample, ada_w, ada_b, pre_norm_g, w_in, q_norm_g, w_uq, kv_norm_g, w_ukv, w_out, post_norm_g):
    depth = ada_w.shape[0]
    bp, tp, d = x_prompt.shape
    bs, ts, _ = x_sample.shape
    past_len = cache_sb_k.shape[2]
    n_sb = cache_sb_k.shape[3]

    tabs_p = _rope_tables(jnp.arange(tp, dtype=jnp.int32))
    tabs_s = _rope_tables(past_len + jnp.arange(ts, dtype=jnp.int32))
    c_all = jnp.concatenate([c_prompt, c_sample], axis=0)

    yp, ys = x_prompt, x_sample
    new_p, new_s = [], []
    for l in range(depth):
        pw = _prep_weights(pre_norm_g[l], w_in[l], q_norm_g[l], w_uq[l], kv_norm_g[l], w_ukv[l])
        w_out16 = w_out[l].astype(BF16)
        post_g = post_norm_g[l][None, :]
        mod = _mod_call(c_all, ada_w[l].astype(BF16), ada_b[l][None, :])
        mod_p = mod[:bp].reshape(bp, 3, d)
        mod_s = mod[bp:].reshape(bs, 3, d)

        (sbq, sbk32, sbk16, sbv32, sbvt, gs, qcat, ckv, kcat, mvt, kr, _) = _proj_call(
            yp, mod_p, pw, tabs_p, PROJ_TM, PROMPT_T)
        sbo = _sb_prompt_call(sbq, sbk16, sbvt, gs, PROMPT_T)
        mlao = _mla_prompt_call(qcat, kcat, mvt, gs, PROMPT_T)
        yp = _out_call(sbo, mlao, yp, mod_p, w_out16, post_g, OUT_TM)
        new_p.append((sbk32.reshape(bp, tp, n_sb, SB_HEAD_DIM), sbv32.reshape(bp, tp, n_sb, SB_HEAD_DIM),
                      ckv, kr))

        (sbq, sbk32, sbk16, sbv32, sbvt, gs, qcat, ckv, _, _, kr, krp) = _proj_call(
            ys, mod_s, pw, tabs_s, ts, ts)
        ktp = jnp.transpose(cache_sb_k[l], (0, 2, 3, 1)).reshape(bs, n_sb * SB_HEAD_DIM, past_len)
        vtp = jnp.transpose(cache_sb_v[l], (0, 2, 3, 1)).reshape(bs, n_sb * SB_HEAD_DIM, past_len)
        sbo = _sb_sample_call(sbq, sbk16, sbvt, ktp, vtp, gs, PAST_TK)
        krt_p = jnp.transpose(cache_mla_krope[l], (0, 2, 1))
        mlao = _mla_sample_call(qcat, ckv, krp, cache_mla_ckv[l], krt_p, gs, pw, PAST_TK, past_len)
        ys = _out_call(sbo, mlao, ys, mod_s, w_out16, post_g, ts)
        new_s.append((sbk32.reshape(bs, ts, n_sb, SB_HEAD_DIM), sbv32.reshape(bs, ts, n_sb, SB_HEAD_DIM),
                      ckv, kr))

    stack = lambda items, i: items[0][i][None] if depth == 1 else jnp.stack([it[i] for it in items])
    return (yp, ys,
            stack(new_p, 0), stack(new_p, 1), stack(new_p, 2), stack(new_p, 3),
            stack(new_s, 0), stack(new_s, 1), stack(new_s, 2), stack(new_s, 3))
```

```python
import functools

import jax
import jax.numpy as jnp
from jax import lax
from jax.experimental import pallas as pl
from jax.experimental.pallas import tpu as pltpu

F32 = jnp.float32
BF16 = jnp.bfloat16

CHUNK = 64
SB_HEAD_DIM = 64
MLA_V_DIM = 64
MLA_NOPE_DIM = 64
MLA_ROPE_DIM = 32
ROPE_THETA = 10000.0
EPS = 1e-6
LOG2E = 1.4426950408889634
LANES = 128
HEAD = 64
MASKED_SCORE = -1e30

VMEM_LIMIT = 56 * 1024 * 1024


def _cparams(n_grid):
    return pltpu.CompilerParams(dimension_semantics=("arbitrary",) * n_grid,
                                vmem_limit_bytes=VMEM_LIMIT)


def _silu(x):
    return x * (1.0 / (1.0 + jnp.exp(-x)))


def _rms(x, g):
    return x * lax.rsqrt(jnp.mean(x * x, axis=-1, keepdims=True) + EPS) * g


def _dot(a, b):
    return jnp.dot(a, b, preferred_element_type=F32)


def _dot_nt(a, b):
    return lax.dot_general(a, b, (((1,), (1,)), ((), ())), preferred_element_type=F32)


def _mod_kernel(c_ref, w_ref, b_ref, o_ref):
    s = _silu(c_ref[...]).astype(BF16)
    o_ref[...] = _dot(s, w_ref[...]) + b_ref[...]


def _mod_call(c_all, ada_w16, ada_b):
    n, d3 = c_all.shape[0], ada_w16.shape[1]
    return pl.pallas_call(
        _mod_kernel,
        out_shape=jax.ShapeDtypeStruct((n, d3), F32),
        compiler_params=pltpu.CompilerParams(vmem_limit_bytes=VMEM_LIMIT),
        name="ada_mod",
    )(c_all, ada_w16, ada_b)


_C_SBQ, _C_SBK, _C_SBV, _C_SBG = 0, 512, 1024, 1536
_C_CQ, _C_CKV, _C_MLAG, _C_KR, _C_END = 2048, 2432, 2688, 3200, 3328
_HALF_ROPE = MLA_ROPE_DIM // 2


def _store_key_blocks_transposed(ref, v):
    tb = ref.shape[3]
    for i in range(ref.shape[1]):
        ref[0, i] = v[i * tb:(i + 1) * tb].T.astype(BF16)


def _proj_kernel(x_ref, mod_ref, pg_ref, w_ref, qg_ref, wa_ref, kvg_ref, wkn_ref, wv_ref,
                 cq_ref, s1_ref, s2_ref, ck_ref, sk_ref,
                 sbq_ref, sbk32_ref, sbk16_ref, sbv32_ref, sbvt_ref, gs_ref, qcat_ref,
                 ckv_ref, kcat_ref, mvt_ref, kr_ref, krp_ref):
    x = x_ref[0]
    rs = lax.rsqrt(jnp.mean(x * x, axis=-1, keepdims=True) + EPS)
    g = pg_ref[...] * (1.0 + mod_ref[0, 1:2, :])
    hb = ((x * rs) * g + mod_ref[0, 0:1, :]).astype(BF16)

    def proj(a, b):
        return _dot(hb, w_ref[:, a:b])

    cq = proj(_C_CQ, _C_CKV)
    ckv = proj(_C_CKV, _C_MLAG)
    r = proj(_C_KR, _C_END)

    v = proj(_C_SBV, _C_SBG)
    sbv32_ref[0] = v
    _store_key_blocks_transposed(sbvt_ref, v)
    gs_ref[0, :, 0:512] = _silu(proj(_C_SBG, _C_CQ)).astype(BF16)
    gs_ref[0, :, 512:1024] = _silu(proj(_C_MLAG, _C_KR)).astype(BF16)

    qa = _dot(_rms(cq, qg_ref[...]).astype(BF16), wa_ref[...])
    cq_t, s1_t, s2_t = cq_ref[...], s1_ref[...], s2_ref[...]
    n_heads = qcat_ref.shape[2] // LANES
    for h in range(n_heads):
        a = qa[:, h * LANES:(h + 1) * LANES]
        up = pltpu.roll(a, _HALF_ROPE, 1)
        down = pltpu.roll(a, LANES - _HALF_ROPE, 1)
        qcat_ref[0, :, h * LANES:(h + 1) * LANES] = (a * cq_t + up * s1_t + down * s2_t).astype(BF16)

    ckvn = _rms(ckv, kvg_ref[...])
    ckv_ref[0] = ckvn
    cb = ckvn.astype(BF16)
    _store_key_blocks_transposed(mvt_ref, _dot(cb, wv_ref[...]))
    kro = r * ck_ref[...] + pltpu.roll(r, LANES - MLA_ROPE_DIM, 1) * sk_ref[...]
    kr_ref[0] = kro[:, :MLA_ROPE_DIM]
    krp_ref[0] = kro.astype(BF16)
    slot = pltpu.roll(kro, MLA_NOPE_DIM, 1)
    kn = _dot(cb, wkn_ref[...])
    for h in range(n_heads):
        kcat_ref[0, :, h * LANES:(h + 1) * LANES] = (kn[:, h * LANES:(h + 1) * LANES] + slot).astype(BF16)

    k = proj(_C_SBK, _C_SBV)
    sbk32_ref[0] = k
    sbk16_ref[0] = k.astype(BF16)
    sbq_ref[0] = (proj(_C_SBQ, _C_SBK) * (SB_HEAD_DIM ** -0.5 * LOG2E)).astype(BF16)


def _proj_call(x, mod, pw, tabs, tm, tb):
    b, t, d = x.shape
    nt = t // tm
    row = lambda n: pl.BlockSpec((1, tm, n), lambda i, j: (i, j, 0))
    colt = pl.BlockSpec((1, tm // tb, 512, tb), lambda i, j: (i, j, 0, 0))
    full = lambda a: pl.BlockSpec(a.shape, lambda i, j: (0,) * a.ndim)
    tab = pl.BlockSpec((tm, LANES), lambda i, j: (j, 0))
    o = lambda n, dt: jax.ShapeDtypeStruct((b, t, n), dt)
    ot = jax.ShapeDtypeStruct((b, t // tb, 512, tb), BF16)
    return pl.pallas_call(
        _proj_kernel,
        grid=(b, nt),
        in_specs=[row(d), pl.BlockSpec((1, 3, d), lambda i, j: (i, 0, 0)), full(pw["pre_g"]),
                  full(pw["w_aug"]), full(pw["q_g"]), full(pw["w_a"]), full(pw["kv_g"]),
                  full(pw["w_kn"]), full(pw["w_v"]), tab, tab, tab, tab, tab],
        out_specs=[row(512), row(512), row(512), row(512), colt, row(1024), row(1024),
                   row(256), row(1024), colt, row(MLA_ROPE_DIM), row(LANES)],
        out_shape=[o(512, BF16), o(512, F32), o(512, BF16), o(512, F32), ot,
                   o(1024, BF16), o(1024, BF16), o(256, F32), o(1024, BF16), ot,
                   o(MLA_ROPE_DIM, F32), o(LANES, BF16)],
        compiler_params=_cparams(2),
        name="in_proj",
    )(x, mod, pw["pre_g"], pw["w_aug"], pw["q_g"], pw["w_a"], pw["kv_g"], pw["w_kn"], pw["w_v"],
      tabs["cq"], tabs["s1"], tabs["s2"], tabs["ck"], tabs["sk"])


def _emit_pipelined(tasks, lag):
    n_slots = max(j + (len(stages) - 1) * lag for j, stages in enumerate(tasks)) + 1
    for slot in range(n_slots):
        for j, stages in enumerate(tasks):
            s, rem = divmod(slot - j, lag)
            if slot >= j and rem == 0 and s < len(stages):
                stages[s]()


def _sb_task(state, c, get_q, get_k, get_vt, lmat, masked, skip_corner=False):
    t = {}

    def log_rem(z):
        low = jnp.minimum(z, 0.0)
        neg_relu = low - z
        l = jnp.log2(1.0 + jnp.exp2(low + neg_relu))
        return (neg_relu - l).astype(BF16)

    def scores():
        t["z"] = _dot_nt(get_k(), get_q())

    def logs():
        z = t.pop("z")
        if masked is None:
            t["z"] = [z]
            lr = log_rem(z)
        elif not skip_corner:
            z = jnp.where(masked, z, MASKED_SCORE)
            t["z"] = [z]
            lr = log_rem(z)
        else:
            h = z.shape[0] // 2
            top = jnp.where(masked[:h], z[:h], MASKED_SCORE)
            corner = jnp.where(masked[h:, h:], z[h:, h:], MASKED_SCORE)
            t["z"] = [top, corner]
            lr = jnp.concatenate(
                [log_rem(top), jnp.concatenate([jnp.zeros((h, h), BF16), log_rem(corner)], axis=1)],
                axis=0)
        t["tail"] = _dot(lmat, lr)

    def weights():
        tail = t.pop("tail")
        zs = t.pop("z")
        if len(zs) == 1:
            w = jnp.exp2(zs[0] + tail).astype(BF16)
        else:
            h = zs[0].shape[0]
            w_top = jnp.exp2(zs[0] + tail[:h]).astype(BF16)
            w_corner = jnp.exp2(zs[1] + tail[h:, h:]).astype(BF16)
            w = jnp.concatenate(
                [w_top, jnp.concatenate([jnp.zeros((h, h), BF16), w_corner], axis=1)], axis=0)
        r = state["r"][c]
        t["scale"] = jnp.exp2(r)
        state["r"][c] = r + tail[0:1, :]
        t["pv"] = _dot(get_vt(), w)

    def accumulate():
        state["acc"][c] = state["acc"][c] + t.pop("pv") * t.pop("scale")

    return [scores, logs, weights, accumulate]


def _causal_units(nt):
    return [(i, i - dist) for dist in range(nt) for i in range(dist, nt)]


def _store_gated(o_ref, g_ref, rows, dims_by_query):
    o_ref[0, rows, :] = (dims_by_query.T * g_ref[0, rows, :].astype(F32)).astype(BF16)


def _sb_prompt_kernel(q_ref, k_ref, vt_ref, l_ref, g_ref, o_ref, *, tb):
    nt = q_ref.shape[1] // tb
    n = q_ref.shape[2] // HEAD
    key = lax.broadcasted_iota(jnp.int32, (tb, tb), 0)
    qry = lax.broadcasted_iota(jnp.int32, (tb, tb), 1)
    causal = key < qry
    lmat = l_ref[...]
    state = {"r": [jnp.zeros((1, tb), F32)] * (n * nt), "acc": [jnp.zeros((HEAD, tb), F32)] * (n * nt)}
    qs = {}

    def unit(i, j, h):
        def get_q():
            if (i, h) not in qs:
                rows = _group_rows(q_ref[0, i * tb:(i + 1) * tb, :], n)
                for hh in range(n):
                    qs[(i, hh)] = rows[hh * tb:(hh + 1) * tb]
            return qs[(i, h)]

        def write_tile():
            group = jnp.concatenate([state["acc"][n * i + hh] for hh in range(n)], axis=0)
            _store_gated(o_ref, g_ref, slice(i * tb, (i + 1) * tb), group)

        get_k = lambda: k_ref[0, j * tb:(j + 1) * tb, :]
        get_vt = lambda: vt_ref[0, j, h * HEAD:(h + 1) * HEAD, :]
        stages = _sb_task(state, n * i + h, get_q, get_k, get_vt, lmat,
                          causal if i == j else None, skip_corner=True)
        return stages + [write_tile] if (j == 0 and h == n - 1) else stages

    _emit_pipelined([unit(i, j, h) for i, j in _causal_units(nt) for h in range(n)], SB_LAG)


def _mla_prompt_kernel(q_ref, k_ref, vt_ref, g_ref, o_ref, *, tb):
    nt = q_ref.shape[1] // tb
    n = q_ref.shape[2] // LANES
    key = lax.broadcasted_iota(jnp.int32, (tb, tb), 0)
    qry = lax.broadcasted_iota(jnp.int32, (tb, tb), 1)
    visible = (key // CHUNK) <= (qry // CHUNK)
    state = _mla_state(n * nt, HEAD, tb)

    def unit(i, j, h):
        def write_tile():
            accs = [state["acc"][n * i + hh] for hh in range(n)]
            outs = [a[:HEAD] * (1.0 / a[HEAD:HEAD + 1]) for a in accs]
            _store_gated(o_ref, g_ref, slice(i * tb, (i + 1) * tb), jnp.concatenate(outs, axis=0))

        get_q = lambda: q_ref[0, i * tb:(i + 1) * tb, h * LANES:(h + 1) * LANES]
        get_k = lambda: k_ref[0, j * tb:(j + 1) * tb, h * LANES:(h + 1) * LANES]
        get_vt = lambda: vt_ref[0, j, h * HEAD:(h + 1) * HEAD, :]
        stages = _mla_task(state, n * i + h, get_q, get_k, get_vt,
                           visible if i == j else None, skip_corner=True)
        return stages + [write_tile] if (j == 0 and h == n - 1) else stages

    _emit_pipelined([unit(i, j, h) for i, j in _causal_units(nt) for h in range(n)], MLA_LAG)


def _suffix_matrix(n):
    s = lax.broadcasted_iota(jnp.int32, (n, n), 0)
    j = lax.broadcasted_iota(jnp.int32, (n, n), 1)
    return (j >= s).astype(BF16)


def _sb_prompt_call(q, k, vt, gs, tb):
    b, t, w = q.shape
    gw = SB_GROUP * HEAD
    pair = pl.BlockSpec((1, t, gw), lambda i, h: (i, 0, h))
    return pl.pallas_call(
        functools.partial(_sb_prompt_kernel, tb=tb),
        grid=(b, w // gw),
        in_specs=[pair, pair, pl.BlockSpec((1, t // tb, gw, tb), lambda i, h: (i, 0, h, 0)),
                  pl.BlockSpec((tb, tb), lambda i, h: (0, 0)), pair],
        out_specs=pair,
        out_shape=jax.ShapeDtypeStruct((b, t, w), BF16),
        compiler_params=_cparams(2),
        name="sb_prompt",
    )(q, k, vt, _suffix_matrix(tb), gs)


def _mla_prompt_call(q, k, vt, gs, tb):
    b, t, w2 = q.shape
    w = w2 // 2
    gw = MLA_GROUP * HEAD
    n_groups = w // gw
    slots = pl.BlockSpec((1, t, MLA_GROUP * LANES), lambda i, h: (i, 0, h))
    return pl.pallas_call(
        functools.partial(_mla_prompt_kernel, tb=tb),
        grid=(b, n_groups),
        in_specs=[slots, slots, pl.BlockSpec((1, t // tb, gw, tb), lambda i, h: (i, 0, h, 0)),
                  pl.BlockSpec((1, t, gw), lambda i, h: (i, 0, n_groups + h))],
        out_specs=pl.BlockSpec((1, t, gw), lambda i, h: (i, 0, h)),
        out_shape=jax.ShapeDtypeStruct((b, t, w), BF16),
        compiler_params=_cparams(2),
        name="mla_prompt",
    )(q, k, vt, gs)


def _group_rows(qg, n):
    lane = lax.broadcasted_iota(jnp.int32, qg.shape, 1)
    qf = qg.astype(F32)
    return jnp.concatenate(
        [jnp.where(lane // HEAD == h, qf, 0.0).astype(BF16) for h in range(n)], axis=0)


def _group_out(acc, n, t):
    acc_t = acc.T
    lane = lax.broadcasted_iota(jnp.int32, (t, acc.shape[0]), 1)
    out = acc_t[:t]
    for h in range(1, n):
        out = jnp.where(lane // HEAD == h, acc_t[h * t:(h + 1) * t], out)
    return out


def _sb_sample_kernel(q_ref, kn_ref, vtn_ref, ktp_ref, vtp_ref, ls_ref, l_ref, g_ref, o_ref, *, tk):
    tq, width = q_ref.shape[1], q_ref.shape[2]
    n = width // HEAD
    n_past = ktp_ref.shape[2] // tk
    qrows = _group_rows(q_ref[0], n)
    key = lax.broadcasted_iota(jnp.int32, (tq, n * tq), 0)
    qry = lax.broadcasted_iota(jnp.int32, (tq, n * tq), 1) % tq
    causal = key < qry
    lmat = l_ref[...]
    state = {"r": [jnp.zeros((1, n * tq), F32)], "acc": [jnp.zeros((width, n * tq), F32)]}
    get_q = lambda: qrows
    tasks = [_sb_task(state, 0, get_q, lambda: kn_ref[0], lambda: vtn_ref[0, 0], ls_ref[...], causal)]
    for kb in reversed(range(n_past)):
        get_k = lambda kb=kb: ktp_ref[0, :, kb * tk:(kb + 1) * tk].T.astype(BF16)
        get_vt = lambda kb=kb: vtp_ref[0, :, kb * tk:(kb + 1) * tk].astype(BF16)
        tasks.append(_sb_task(state, 0, get_q, get_k, get_vt, lmat, None))
    _emit_pipelined(tasks, SB_SAMPLE_LAG)
    o_ref[0] = (_group_out(state["acc"][0], n, tq) * g_ref[0].astype(F32)).astype(BF16)


def _sb_sample_call(q, kn, vtn, ktp, vtp, gs, tk):
    b, t, w = q.shape
    tp = ktp.shape[2]
    new = pl.BlockSpec((1, t, SAMPLE_GROUP), lambda i, h: (i, 0, h))
    past = pl.BlockSpec((1, SAMPLE_GROUP, tp), lambda i, h: (i, h, 0))
    return pl.pallas_call(
        functools.partial(_sb_sample_kernel, tk=tk),
        grid=(b, w // SAMPLE_GROUP),
        in_specs=[new, new, pl.BlockSpec((1, 1, SAMPLE_GROUP, t), lambda i, h: (i, 0, h, 0)),
                  past, past,
                  pl.BlockSpec((t, t), lambda i, h: (0, 0)),
                  pl.BlockSpec((tk, tk), lambda i, h: (0, 0)), new],
        out_specs=new,
        out_shape=jax.ShapeDtypeStruct((b, t, w), BF16),
        compiler_params=_cparams(2),
        name="sb_sample",
    )(q, kn, vtn, ktp, vtp, _suffix_matrix(t), _suffix_matrix(tk), gs)


def _mla_task(state, c, get_q, get_k, get_vt, masked, skip_corner=False):
    t = {}

    def scores():
        t["s"] = _dot_nt(get_k(), get_q())

    def probs():
        s = t.pop("s")
        m = state["m"][c]
        if masked is None or not skip_corner:
            if masked is not None:
                s = jnp.where(masked, s, MASKED_SCORE)
            m_new = jnp.maximum(m, jnp.max(s, axis=0, keepdims=True))
            p = jnp.exp2(s - m_new).astype(BF16)
        else:
            h = s.shape[0] // 2
            top = jnp.where(masked[:h], s[:h], MASKED_SCORE)
            corner = jnp.where(masked[h:, h:], s[h:, h:], MASKED_SCORE)
            top_max = jnp.max(top, axis=0, keepdims=True)
            corner_max = jnp.max(corner, axis=0, keepdims=True)
            tile_max = jnp.concatenate(
                [top_max[:, :h], jnp.maximum(top_max[:, h:], corner_max)], axis=1)
            m_new = jnp.maximum(m, tile_max)
            p_top = jnp.exp2(top - m_new).astype(BF16)
            p_corner = jnp.exp2(corner - m_new[:, h:]).astype(BF16)
            p = jnp.concatenate(
                [p_top, jnp.concatenate([jnp.zeros((h, h), BF16), p_corner], axis=1)], axis=0)
        t["alpha"] = jnp.exp2(m - m_new)
        state["m"][c] = m_new
        t["pv"] = _dot(_with_sum_row(get_vt()), p)

    def accumulate():
        state["acc"][c] = t.pop("alpha") * state["acc"][c] + t.pop("pv")

    return [scores, probs, accumulate]


SUM_ROWS = 16


def _with_sum_row(vt):
    row = lax.broadcasted_iota(jnp.int32, (SUM_ROWS, vt.shape[1]), 0)
    return jnp.concatenate([vt, (row == 0).astype(vt.dtype)], axis=0)


def _mla_state(n, nd, nq):
    return {"m": [jnp.full((1, nq), MASKED_SCORE, F32)] * n,
            "acc": [jnp.zeros((nd + SUM_ROWS, nq), F32)] * n}


def _mla_sample_kernel(q_ref, cn_ref, krn_ref, cp_ref, krtp_ref, wknt_ref, sel_ref, wvf_ref, g_ref,
                       o_ref, *, tk):
    tq = q_ref.shape[1]
    n_heads = wknt_ref.shape[0]
    n_past = cp_ref.shape[1] // tk
    slots = [q_ref[0, :, h * LANES:(h + 1) * LANES] for h in range(n_heads)]
    qa = jnp.concatenate([_dot(s, wknt_ref[h]) for h, s in enumerate(slots)], axis=0).astype(BF16)
    qr = jnp.concatenate([_dot(s, sel_ref[...]) for s in slots], axis=0).astype(BF16)
    nq = n_heads * tq
    state = {"m": jnp.full((nq, 1), MASKED_SCORE, F32), "l": jnp.zeros((nq, 1), F32),
             "acc": jnp.zeros((nq, qa.shape[1]), F32)}

    def unit(get_latent, rope_scores):
        t = {}

        def scores():
            t["lat"] = get_latent()
            t["s"] = _dot_nt(qa, t["lat"]) + rope_scores()

        def probs():
            s = t.pop("s")
            m = state["m"]
            m_new = jnp.maximum(m, jnp.max(s, axis=1, keepdims=True))
            p = jnp.exp2(s - m_new)
            alpha = jnp.exp2(m - m_new)
            state["m"] = m_new
            state["l"] = alpha * state["l"] + jnp.sum(p, axis=1, keepdims=True)
            t["alpha"] = alpha
            t["pv"] = _dot(p.astype(BF16), t.pop("lat"))

        def accumulate():
            state["acc"] = t.pop("alpha") * state["acc"] + t.pop("pv")

        return [scores, probs, accumulate]

    tasks = [unit(lambda: cn_ref[0].astype(BF16), lambda: _dot_nt(qr, krn_ref[0]))]
    pad = jnp.zeros((LANES - MLA_ROPE_DIM, tk), BF16)
    for kb in range(n_past):
        get_latent = lambda kb=kb: cp_ref[0, kb * tk:(kb + 1) * tk, :].astype(BF16)
        rope_scores = lambda kb=kb: _dot(qr, jnp.concatenate(
            [krtp_ref[0, :, kb * tk:(kb + 1) * tk].astype(BF16), pad], axis=0))
        tasks.append(unit(get_latent, rope_scores))
    _emit_pipelined(tasks, MLA_SAMPLE_LAG)
    lat_out = (state["acc"] * (1.0 / state["l"])).astype(BF16)
    out = _dot(lat_out[0:tq], wvf_ref[0])
    for h in range(1, n_heads):
        out = out + _dot(lat_out[h * tq:(h + 1) * tq], wvf_ref[h])
    o_ref[0] = (out * g_ref[0].astype(F32)).astype(BF16)


def _mla_sample_call(q, ckv_new, kr_new, ckv_past, krt_past, gs, pw, tk, past_len):
    b, t, w = q.shape
    tp, r_kv = ckv_past.shape[1], ckv_past.shape[2]
    wo = pw["w_vf"].shape[2]
    assert gs.shape[2] == 2 * wo
    assert past_len % CHUNK == 0 and t <= CHUNK
    full = lambda a: pl.BlockSpec(a.shape, lambda i: (0,) * a.ndim)
    return pl.pallas_call(
        functools.partial(_mla_sample_kernel, tk=tk),
        grid=(b,),
        in_specs=[pl.BlockSpec((1, t, w), lambda i: (i, 0, 0)),
                  pl.BlockSpec((1, t, r_kv), lambda i: (i, 0, 0)),
                  pl.BlockSpec((1, t, LANES), lambda i: (i, 0, 0)),
                  pl.BlockSpec((1, tp, r_kv), lambda i: (i, 0, 0)),
                  pl.BlockSpec((1, MLA_ROPE_DIM, tp), lambda i: (i, 0, 0)),
                  full(pw["w_knt"]), full(pw["rope_sel"]), full(pw["w_vf"]),
                  pl.BlockSpec((1, t, wo), lambda i: (i, 0, 1))],
        out_specs=pl.BlockSpec((1, t, wo), lambda i: (i, 0, 0)),
        out_shape=jax.ShapeDtypeStruct((b, t, wo), BF16),
        compiler_params=_cparams(1),
        name="mla_sample",
    )(q, ckv_new, kr_new, ckv_past, krt_past, pw["w_knt"], pw["rope_sel"], pw["w_vf"], gs)


def _out_kernel(sbo_ref, mlao_ref, x_ref, mod_ref, w_ref, g_ref, y_ref):
    half = sbo_ref.shape[2]
    out = _dot(sbo_ref[0], w_ref[:half, :]) + _dot(mlao_ref[0], w_ref[half:, :])
    y_ref[0] = x_ref[0] + mod_ref[0, 2:3, :] * _rms(out, g_ref[...])


def _out_call(sbo, mlao, x, mod, w_out16, post_g, tm):
    b, t, d = x.shape
    row = lambda n: pl.BlockSpec((1, tm, n), lambda i, j: (i, j, 0))
    full = lambda a: pl.BlockSpec(a.shape, lambda i, j: (0,) * a.ndim)
    return pl.pallas_call(
        _out_kernel,
        grid=(b, t // tm),
        in_specs=[row(512), row(512), row(d),
                  pl.BlockSpec((1, 3, d), lambda i, j: (i, 0, 0)), full(w_out16), full(post_g)],
        out_specs=row(d),
        out_shape=jax.ShapeDtypeStruct((b, t, d), F32),
        compiler_params=_cparams(2),
        name="out_proj",
    )(sbo, mlao, x, mod, w_out16, post_g)


def _prep_weights(pre_g, w_in, q_g, w_uq, kv_g, w_ukv):
    d = w_in.shape[0]
    n_heads = w_ukv.shape[1] // (MLA_NOPE_DIM + MLA_V_DIM)
    half = MLA_ROPE_DIM // 2
    sizes = (512, 512, 512, 512, w_uq.shape[0], w_ukv.shape[0], MLA_ROPE_DIM, 512)
    starts = [0]
    for s in sizes:
        starts.append(starts[-1] + s)
    cols = lambda i: w_in[:, starts[i]:starts[i + 1]]
    kr = cols(6)
    kr_swapped = jnp.concatenate([-kr[:, half:], kr[:, :half]], axis=1)
    zpad = jnp.zeros((d, LANES - 2 * MLA_ROPE_DIM), w_in.dtype)
    w_aug = jnp.concatenate([cols(0), cols(1), cols(2), cols(3), cols(4), cols(5), cols(7),
                             kr, kr_swapped, zpad], axis=1).astype(BF16)
    assert w_aug.shape[1] == _C_END

    r_q = w_uq.shape[0]
    uq = w_uq.reshape(r_q, n_heads, MLA_NOPE_DIM + MLA_ROPE_DIM)
    zq = jnp.zeros((r_q, n_heads, LANES - MLA_NOPE_DIM - MLA_ROPE_DIM), w_uq.dtype)
    w_a = jnp.concatenate([uq, zq], axis=-1).reshape(r_q, n_heads * LANES).astype(BF16)

    r_kv = w_ukv.shape[0]
    ukv = w_ukv.reshape(r_kv, n_heads, MLA_NOPE_DIM + MLA_V_DIM)
    uk, uv = ukv[..., :MLA_NOPE_DIM], ukv[..., MLA_NOPE_DIM:]
    zk = jnp.zeros((r_kv, n_heads, LANES - MLA_NOPE_DIM), w_ukv.dtype)
    w_kn = jnp.concatenate([uk, zk], axis=-1).reshape(r_kv, n_heads * LANES).astype(BF16)
    w_v = uv.reshape(r_kv, n_heads * MLA_V_DIM).astype(BF16)

    w_knt = jnp.transpose(jnp.concatenate([uk, zk], axis=-1), (1, 2, 0)).astype(BF16)
    eye = jnp.eye(n_heads, dtype=w_ukv.dtype)
    w_vf = jnp.einsum("rhd,hg->hrgd", uv, eye).reshape(n_heads, r_kv, n_heads * MLA_V_DIM).astype(BF16)
    src = lax.broadcasted_iota(jnp.int32, (LANES, LANES), 0)
    dst = lax.broadcasted_iota(jnp.int32, (LANES, LANES), 1)
    rope_sel = ((src == dst + MLA_NOPE_DIM) & (dst < MLA_ROPE_DIM)).astype(BF16)

    return dict(pre_g=pre_g[None, :], w_aug=w_aug, q_g=q_g[None, :], w_a=w_a,
                kv_g=kv_g[None, :], w_kn=w_kn, w_v=w_v, w_knt=w_knt, w_vf=w_vf, rope_sel=rope_sel)


def _rope_tables(pos):
    r = MLA_ROPE_DIM
    half = r // 2
    inv_freq = ROPE_THETA ** (-jnp.arange(0, r, 2, dtype=F32) / r)
    ang = pos.astype(F32)[:, None] * inv_freq[None, :]
    cos, sin = jnp.cos(ang), jnp.sin(ang)
    t = pos.shape[0]
    scale = (MLA_NOPE_DIM + MLA_ROPE_DIM) ** -0.5 * LOG2E
    z = lambda n: jnp.zeros((t, n), F32)
    cq = jnp.concatenate([jnp.full((t, MLA_NOPE_DIM), scale, F32), cos * scale, cos * scale,
                          z(LANES - MLA_NOPE_DIM - r)], axis=1)
    s1 = jnp.concatenate([z(MLA_NOPE_DIM + half), sin * scale, z(LANES - MLA_NOPE_DIM - r)], axis=1)
    s2 = jnp.concatenate([z(MLA_NOPE_DIM), -sin * scale, z(LANES - MLA_NOPE_DIM - half)], axis=1)
    ck = jnp.concatenate([cos, cos, z(LANES - r)], axis=1)
    sk = jnp.concatenate([sin, sin, z(LANES - r)], axis=1)
    return dict(cq=cq, s1=s1, s2=s2, ck=ck, sk=sk)


PROMPT_T = 256
SB_GROUP = 2
MLA_GROUP = 4
PROJ_TM = 512
OUT_TM = 1024
PAST_TK = 256
SAMPLE_GROUP = 256
SB_LAG = 2
MLA_LAG = 6
SB_SAMPLE_LAG = 4
MLA_SAMPLE_LAG = 3


def kernel(x_prompt, x_sample, cache_sb_k, cache_sb_v, cache_mla_ckv, cache_mla_krope, c_prompt, c_sample, ada_w, ada_b, pre_norm_g, w_in, q_norm_g, w_uq, kv_norm_g, w_ukv, w_out, post_norm_g):
    depth = ada_w.shape[0]
    bp, tp, d = x_prompt.shape
    bs, ts, _ = x_sample.shape
    past_len = cache_sb_k.shape[2]
    n_sb = cache_sb_k.shape[3]

    tabs_p = _rope_tables(jnp.arange(tp, dtype=jnp.int32))
    tabs_s = _rope_tables(past_len + jnp.arange(ts, dtype=jnp.int32))
    c_all = jnp.concatenate([c_prompt, c_sample], axis=0)

    yp, ys = x_prompt, x_sample
    new_p, new_s = [], []
    for l in range(depth):
        pw = _prep_weights(pre_norm_g[l], w_in[l], q_norm_g[l], w_uq[l], kv_norm_g[l], w_ukv[l])
        w_out16 = w_out[l].astype(BF16)
        post_g = post_norm_g[l][None, :]
        mod = _mod_call(c_all, ada_w[l].astype(BF16), ada_b[l][None, :])
        mod_p = mod[:bp].reshape(bp, 3, d)
        mod_s = mod[bp:].reshape(bs, 3, d)

        (sbq, sbk32, sbk16, sbv32, sbvt, gs, qcat, ckv, kcat, mvt, kr, _) = _proj_call(
            yp, mod_p, pw, tabs_p, PROJ_TM, PROMPT_T)
        sbo = _sb_prompt_call(sbq, sbk16, sbvt, gs, PROMPT_T)
        mlao = _mla_prompt_call(qcat, kcat, mvt, gs, PROMPT_T)
        yp = _out_call(sbo, mlao, yp, mod_p, w_out16, post_g, OUT_TM)
        new_p.append((sbk32.reshape(bp, tp, n_sb, SB_HEAD_DIM), sbv32.reshape(bp, tp, n_sb, SB_HEAD_DIM),
                      ckv, kr))

        (sbq, sbk32, sbk16, sbv32, sbvt, gs, qcat, ckv, _, _, kr, krp) = _proj_call(
            ys, mod_s, pw, tabs_s, ts, ts)
        ktp = jnp.transpose(cache_sb_k[l], (0, 2, 3, 1)).reshape(bs, n_sb * SB_HEAD_DIM, past_len)
        vtp = jnp.transpose(cache_sb_v[l], (0, 2, 3, 1)).reshape(bs, n_sb * SB_HEAD_DIM, past_len)
        sbo = _sb_sample_call(sbq, sbk16, sbvt, ktp, vtp, gs, PAST_TK)
        krt_p = jnp.transpose(cache_mla_krope[l], (0, 2, 1))
        mlao = _mla_sample_call(qcat, ckv, krp, cache_mla_ckv[l], krt_p, gs, pw, PAST_TK, past_len)
        ys = _out_call(sbo, mlao, ys, mod_s, w_out16, post_g, ts)
        new_s.append((sbk32.reshape(bs, ts, n_sb, SB_HEAD_DIM), sbv32.reshape(bs, ts, n_sb, SB_HEAD_DIM),
                      ckv, kr))

    stack = lambda items, i: items[0][i][None] if depth == 1 else jnp.stack([it[i] for it in items])
    return (yp, ys,
            stack(new_p, 0), stack(new_p, 1), stack(new_p, 2), stack(new_p, 3),
            stack(new_s, 0), stack(new_s, 1), stack(new_s, 2), stack(new_s, 3))
```
